```python
import jax
import jax.numpy as jnp
from jax import lax
import numpy as np

D_MODEL = 1024
BATCH = 8
SEQ = 4096
DEPTH = 2

GRID_W = 64
CTX_LEN = 256
EPS = 1e-6
N_MOD = 6

A_HEADS = 8
A_KV_HEADS = 2
A_GROUP = A_HEADS // A_KV_HEADS
A_HEAD_DIM = 64
A_WIDTH = A_HEADS * A_HEAD_DIM
A_KV_WIDTH = A_KV_HEADS * A_HEAD_DIM
A_SCALE = A_HEAD_DIM ** -0.5
ROPE_THETA = 10000.0
ROPE_FREQS = A_HEAD_DIM // 4
Q_BLOCK = 128

B_GROUPS = 4
B_WIDTH = D_MODEL // 2
B_GROUP_DIM = B_WIDTH // B_GROUPS
POOL_WINDOWS = (2, 4, 8, 16)

EVEN_IN = A_WIDTH + 2 * A_KV_WIDTH + B_WIDTH
EVEN_OUT = A_WIDTH + B_WIDTH

C_HEADS = 16
C_HEAD_DIM = D_MODEL // C_HEADS
C_WIDTH = C_HEADS * C_HEAD_DIM
C_SCALE = C_HEAD_DIM ** -0.5
NA_ROWS_MAX = 8
NA_COLS = 16

N_GROUPS = 4
E_PER_GROUP = 8
N_EXPERTS = N_GROUPS * E_PER_GROUP
TOP_K = 2
D_EXPERT = D_MODEL // 2
MOE_BLOCK = 128

kernel_name = 'hybrid_gqa_pool_natten_hmoe_diffusion'


def rmsnorm(x, g):
    xf = x.astype(jnp.float32)
    y = xf * lax.rsqrt(jnp.mean(xf * xf, axis=-1, keepdims=True) + EPS)
    return (y * g.astype(jnp.float32)).astype(x.dtype)


def modulate(h, shift, scale):
    return h * (1 + scale) + shift


def axial_rope_tables(n_tok):
    t = jnp.arange(n_tok, dtype=jnp.int32)
    pos = jnp.stack([t // GRID_W, t % GRID_W], axis=-1).astype(jnp.float32)
    inv = ROPE_THETA ** (-jnp.arange(ROPE_FREQS, dtype=jnp.float32) / ROPE_FREQS)
    ang = pos[:, :, None] * inv
    return jnp.cos(ang), jnp.sin(ang)


def apply_axial_rope(x, cos, sin):
    sh = x.shape
    xr = x.reshape(sh[:-1] + (2, 2, ROPE_FREQS)).astype(jnp.float32)
    x1, x2 = xr[..., 0, :], xr[..., 1, :]
    out = jnp.stack([x1 * cos - x2 * sin, x1 * sin + x2 * cos], axis=-2)
    return out.reshape(sh).astype(x.dtype)


def gqa_queries(q, gain):
    B, L, _ = q.shape
    q = rmsnorm(q.reshape(B, L, A_KV_HEADS, A_GROUP, A_HEAD_DIM), gain)
    return q.transpose(0, 2, 3, 1, 4)


def gqa_keys(k, gain):
    B, L, _ = k.shape
    return rmsnorm(k.reshape(B, L, A_KV_HEADS, A_HEAD_DIM), gain).transpose(0, 2, 1, 3)


def gqa_values(v):
    B, L, _ = v.shape
    return v.reshape(B, L, A_KV_HEADS, A_HEAD_DIM).transpose(0, 2, 1, 3)


def pool_mixer(u, pool_w, pool_scale):
    B, L, _ = u.shape
    ug = u.reshape(B, L, B_GROUPS, B_GROUP_DIM).astype(jnp.float32)
    cs = jnp.concatenate([jnp.zeros((B, 1, B_GROUPS, B_GROUP_DIM), jnp.float32), jnp.cumsum(ug, axis=1)], axis=1)
    t = jnp.arange(L, dtype=jnp.int32)
    means = []
    for g, w in enumerate(POOL_WINDOWS):
        lo = jnp.clip(t - w // 2, 0, L)
        hi = jnp.clip(t - w // 2 + w, 0, L)
        cnt = (hi - lo).astype(jnp.float32)
        cs_g = cs[:, :, g]
        means.append((cs_g[:, hi] - cs_g[:, lo]) / cnt[None, :, None])
    pooled = (jnp.stack(means, axis=2) - ug).astype(u.dtype)
    mixed = jnp.einsum('blgc,gcd->blgd', pooled, pool_w)
    return mixed.reshape(B, L, B_WIDTH) * pool_scale


def even_mixer(h, hc, w_in, w_out, q_gain, k_gain, pool_w, pool_scale, cos, sin, ctx_out):
    B, S, _ = h.shape
    cuts = (A_WIDTH, A_WIDTH + A_KV_WIDTH, A_WIDTH + 2 * A_KV_WIDTH)
    q, k, v, u = jnp.split(h @ w_in, cuts, axis=-1)
    if ctx_out:
        qc, kc, vc, uc = jnp.split(hc @ w_in, cuts, axis=-1)
    else:
        kc, vc = jnp.split(hc @ w_in[:, A_WIDTH:A_WIDTH + 2 * A_KV_WIDTH], 2, axis=-1)
    kc_h = gqa_keys(kc, k_gain)
    vc_h = gqa_values(vc)
    q_h = apply_axial_rope(gqa_queries(q, q_gain), cos, sin)
    k_h = apply_axial_rope(gqa_keys(k, k_gain), cos, sin)
    k_all = jnp.concatenate([kc_h, k_h], axis=2)
    v_all = jnp.concatenate([vc_h, gqa_values(v)], axis=2)
    nb = S // Q_BLOCK
    qb = q_h.reshape(B, A_KV_HEADS, A_GROUP, nb, Q_BLOCK, A_HEAD_DIM).transpose(3, 0, 1, 2, 4, 5)

    def attend_block(qblk):
        s = jnp.einsum('bhgqd,bhkd->bhgqk', qblk, k_all).astype(jnp.float32) * A_SCALE
        p = jax.nn.softmax(s, axis=-1).astype(v_all.dtype)
        return jnp.einsum('bhgqk,bhkd->bhgqd', p, v_all)

    o = lax.map(attend_block, qb)
    a_lat = o.transpose(1, 0, 4, 2, 3, 5).reshape(B, S, A_WIDTH)
    b_lat = pool_mixer(u, pool_w, pool_scale)
    y = jnp.concatenate([a_lat, b_lat], axis=-1) @ w_out
    if not ctx_out:
        return y, None
    Lc = hc.shape[1]
    qc_h = gqa_queries(qc, q_gain)
    s = jnp.einsum('bhgqd,bhkd->bhgqk', qc_h, kc_h).astype(jnp.float32) * A_SCALE
    p = jax.nn.softmax(s, axis=-1).astype(vc_h.dtype)
    a_ctx = jnp.einsum('bhgqk,bhkd->bhgqd', p, vc_h).transpose(0, 3, 1, 2, 4).reshape(B, Lc, A_WIDTH)
    b_ctx = pool_mixer(uc, pool_w, pool_scale)
    yc = jnp.concatenate([a_ctx, b_ctx], axis=-1) @ w_out
    return y, yc


def c_heads(t):
    B, L, _ = t.shape
    return t.reshape(B, L, C_HEADS, C_HEAD_DIM).transpose(0, 2, 1, 3)


def odd_mixer(h, hc, w_in, w_out, rel_bias, ctx_out):
    B, S, _ = h.shape
    rows = S // GRID_W
    kh = min(NA_ROWS_MAX, rows)
    q, k, v = jnp.split(h @ w_in, 3, axis=-1)
    if ctx_out:
        qc, kc, vc = jnp.split(hc @ w_in, 3, axis=-1)
    else:
        kc, vc = jnp.split(hc @ w_in[:, C_WIDTH:], 2, axis=-1)
    kc_h, vc_h = c_heads(kc), c_heads(vc)

    def grid(t):
        return t.reshape(B, rows, GRID_W, C_HEADS, C_HEAD_DIM).transpose(0, 3, 1, 2, 4)

    qg, kg, vg = grid(q), grid(k), grid(v)
    cols = jnp.arange(GRID_W, dtype=jnp.int32)
    col_start = jnp.clip(cols - NA_COLS // 2, 0, GRID_W - NA_COLS)
    col_idx = col_start[:, None] + jnp.arange(NA_COLS, dtype=jnp.int32)[None, :]
    col_off = col_idx - cols[:, None] + (NA_COLS - 1)
    n_loc = kh * NA_COLS

    def row_block(r):
        rs = jnp.clip(r - kh // 2, 0, rows - kh)
        qr = lax.dynamic_index_in_dim(qg, r, axis=2, keepdims=False)
        kr = lax.dynamic_slice_in_dim(kg, rs, kh, axis=2)
        vr = lax.dynamic_slice_in_dim(vg, rs, kh, axis=2)
        kw = kr[:, :, :, col_idx]
        vw = vr[:, :, :, col_idx]
        row_off = rs + jnp.arange(kh, dtype=jnp.int32) - r + (NA_ROWS_MAX - 1)
        bias = rel_bias[:, row_off][:, :, col_off].transpose(0, 2, 1, 3)
        s_loc = jnp.einsum('bhqd,bhiqjd->bhqij', qr, kw).astype(jnp.float32) * C_SCALE + bias.astype(jnp.float32)
        s_ctx = jnp.einsum('bhqd,bhkd->bhqk', qr, kc_h).astype(jnp.float32) * C_SCALE
        s = jnp.concatenate([s_loc.reshape(B, C_HEADS, GRID_W, n_loc), s_ctx], axis=-1)
        p = jax.nn.softmax(s, axis=-1).astype(vw.dtype)
        p_loc = p[..., :n_loc].reshape(B, C_HEADS, GRID_W, kh, NA_COLS)
        return (jnp.einsum('bhqij,bhiqjd->bhqd', p_loc, vw)
                + jnp.einsum('bhqk,bhkd->bhqd', p[..., n_loc:], vc_h))

    o = lax.map(row_block, jnp.arange(rows, dtype=jnp.int32))
    y = o.transpose(1, 0, 3, 2, 4).reshape(B, S, C_WIDTH) @ w_out
    if not ctx_out:
        return y, None
    Lc = hc.shape[1]
    qc_h = c_heads(qc)
    s = jnp.einsum('bhqd,bhkd->bhqk', qc_h, kc_h).astype(jnp.float32) * C_SCALE
    p = jax.nn.softmax(s, axis=-1).astype(vc_h.dtype)
    yc = jnp.einsum('bhqk,bhkd->bhqd', p, vc_h).transpose(0, 2, 1, 3).reshape(B, Lc, C_WIDTH) @ w_out
    return y, yc


def hier_moe(xt, wg, bg, we, be, w1, w3, w2):
    T, D = xt.shape
    xf = xt.astype(jnp.float32)
    g_logits = xf @ wg.astype(jnp.float32) + bg.astype(jnp.float32)
    g_prob = jax.nn.softmax(g_logits, axis=-1)
    g_idx = jnp.argmax(g_logits, axis=-1)
    g_w = jnp.take_along_axis(g_prob, g_idx[:, None], axis=1)[:, 0]
    e_logits = (xf @ we.astype(jnp.float32) + be.astype(jnp.float32)).reshape(T, N_GROUPS, E_PER_GROUP)
    e_logits = jnp.take_along_axis(e_logits, g_idx[:, None, None], axis=1)[:, 0]
    top_p, top_e = lax.top_k(jax.nn.softmax(e_logits, axis=-1), TOP_K)
    top_p = top_p / jnp.sum(top_p, axis=-1, keepdims=True)
    weights = (g_w[:, None] * top_p).reshape(-1)
    flat_e = (g_idx[:, None] * E_PER_GROUP + top_e).reshape(-1)
    n_assign = T * TOP_K
    order = jnp.argsort(flat_e)
    sorted_e = flat_e[order]
    tok = order // TOP_K
    counts = jnp.bincount(flat_e, length=N_EXPERTS)
    start = jnp.cumsum(counts) - counts
    padded = (counts + MOE_BLOCK - 1) // MOE_BLOCK * MOE_BLOCK
    pend = jnp.cumsum(padded)
    pstart = pend - padded
    dest = pstart[sorted_e] + jnp.arange(n_assign, dtype=jnp.int32) - start[sorted_e]
    n_blocks = -(-n_assign // MOE_BLOCK) + N_EXPERTS
    xpad = jnp.zeros((n_blocks * MOE_BLOCK, D), xt.dtype).at[dest].set(xt[tok])
    block_e = jnp.minimum(jnp.searchsorted(pend, jnp.arange(n_blocks, dtype=jnp.int32) * MOE_BLOCK, side='right'), N_EXPERTS - 1)

    def expert_block(args):
        xb, e = args
        hb = jax.nn.silu(xb @ w1[e]) * (xb @ w3[e])
        return hb @ w2[e]

    ypad = lax.map(expert_block, (xpad.reshape(n_blocks, MOE_BLOCK, D), block_e))
    y = ypad.reshape(-1, D)[dest] * weights[order][:, None].astype(xt.dtype)
    return jnp.zeros_like(xt).at[tok].add(y)


def setup_inputs(seed: int = 0) -> dict:
    key = jax.random.key(seed)
    ks = iter(jax.random.split(key, 32))
    n_even = (DEPTH + 1) // 2
    n_odd = DEPTH // 2
    D = D_MODEL

    def nrm(shape, scale):
        return jax.random.normal(next(ks), shape, jnp.float32) * scale

    return {
        'x': nrm((BATCH, SEQ, D), 1.0),
        'c': nrm((BATCH, D), 1.0),
        'ctx': nrm((BATCH, CTX_LEN, D), 1.0),
        'c_ctx': nrm((D,), 1.0),
        'ada_w': nrm((DEPTH, D, N_MOD * D), 0.5 * D ** -0.5),
        'ada_b': nrm((DEPTH, N_MOD * D), 0.02),
        'norm_mix_g': 1.0 + nrm((DEPTH, D), 0.1),
        'norm_ffn_g': 1.0 + nrm((DEPTH, D), 0.1),
        'even_w_in': nrm((n_even, D, EVEN_IN), D ** -0.5),
        'even_w_out': nrm((n_even, EVEN_OUT, D), EVEN_OUT ** -0.5),
        'a_q_gain': 1.0 + nrm((n_even, A_HEAD_DIM), 0.1),
        'a_k_gain': 1.0 + nrm((n_even, A_HEAD_DIM), 0.1),
        'pool_w': nrm((n_even, B_GROUPS, B_GROUP_DIM, B_GROUP_DIM), B_GROUP_DIM ** -0.5),
        'pool_scale': 1.0 + nrm((n_even, B_WIDTH), 0.1),
        'odd_w_in': nrm((n_odd, D, 3 * C_WIDTH), D ** -0.5),
        'odd_w_out': nrm((n_odd, C_WIDTH, D), C_WIDTH ** -0.5),
        'na_rel_bias': nrm((n_odd, C_HEADS, 2 * NA_ROWS_MAX - 1, 2 * NA_COLS - 1), 0.5),
        'moe_w_group': nrm((DEPTH, D, N_GROUPS), D ** -0.5),
        'moe_b_group': nrm((DEPTH, N_GROUPS), 0.01),
        'moe_w_expert': nrm((DEPTH, D, N_EXPERTS), D ** -0.5),
        'moe_b_expert': nrm((DEPTH, N_EXPERTS), 0.01),
        'moe_w1': nrm((DEPTH, N_EXPERTS, D, D_EXPERT), D ** -0.5),
        'moe_w3': nrm((DEPTH, N_EXPERTS, D, D_EXPERT), D ** -0.5),
        'moe_w2': nrm((DEPTH, N_EXPERTS, D_EXPERT, D), D_EXPERT ** -0.5),
        'final_g': 1.0 + nrm((D,), 0.1),
    }


def reference(x, c, ctx, c_ctx, ada_w, ada_b, norm_mix_g, norm_ffn_g, even_w_in, even_w_out, a_q_gain, a_k_gain,
              pool_w, pool_scale, odd_w_in, odd_w_out, na_rel_bias, moe_w_group, moe_b_group, moe_w_expert,
              moe_b_expert, moe_w1, moe_w3, moe_w2, final_g):
    B, S, D = x.shape
    Lc = ctx.shape[1]
    cos, sin = axial_rope_tables(S)
    for l in range(DEPTH):
        last = l == DEPTH - 1
        mod = jax.nn.silu(c) @ ada_w[l] + ada_b[l]
        mod_c = jax.nn.silu(c_ctx) @ ada_w[l] + ada_b[l]
        sh1, sc1, g1, sh2, sc2, g2 = jnp.split(mod[:, None, :], N_MOD, axis=-1)
        csh1, csc1, cg1, csh2, csc2, cg2 = jnp.split(mod_c, N_MOD, axis=-1)
        h = modulate(rmsnorm(x, norm_mix_g[l]), sh1, sc1)
        hc = modulate(rmsnorm(ctx, norm_mix_g[l]), csh1, csc1)
        if l % 2 == 0:
            i = l // 2
            y, yc = even_mixer(h, hc, even_w_in[i], even_w_out[i], a_q_gain[i], a_k_gain[i], pool_w[i],
                               pool_scale[i], cos, sin, not last)
        else:
            i = l // 2
            y, yc = odd_mixer(h, hc, odd_w_in[i], odd_w_out[i], na_rel_bias[i], not last)
        x = x + g1 * y
        h2 = modulate(rmsnorm(x, norm_ffn_g[l]), sh2, sc2)
        moe_args = (moe_w_group[l], moe_b_group[l], moe_w_expert[l], moe_b_expert[l], moe_w1[l], moe_w3[l], moe_w2[l])
        if not last:
            ctx = ctx + cg1 * yc
            hc2 = modulate(rmsnorm(ctx, norm_ffn_g[l]), csh2, csc2)
            f = hier_moe(jnp.concatenate([h2.reshape(B * S, D), hc2.reshape(B * Lc, D)], axis=0), *moe_args)
            x = x + g2 * f[:B * S].reshape(B, S, D)
            ctx = ctx + cg2 * f[B * S:].reshape(B, Lc, D)
        else:
            f = hier_moe(h2.reshape(B * S, D), *moe_args)
            x = x + g2 * f.reshape(B, S, D)
    return rmsnorm(x, final_g)
```

```python
import functools

import jax
import jax.numpy as jnp
import numpy as np
from jax import lax
from jax.experimental import pallas as pl
from jax.experimental.pallas import tpu as pltpu

D_MODEL = 1024
BATCH = 8
SEQ = 4096
DEPTH = 2
GRID_W = 64
CTX_LEN = 256
EPS = 1e-6
N_MOD = 6

A_HEADS = 8
A_KV_HEADS = 2
A_HEAD_DIM = 64
A_WIDTH = A_HEADS * A_HEAD_DIM
A_KV_WIDTH = A_KV_HEADS * A_HEAD_DIM
A_SCALE = A_HEAD_DIM ** -0.5
ROPE_THETA = 10000.0
ROPE_FREQS = A_HEAD_DIM // 4

B_GROUPS = 4
B_WIDTH = D_MODEL // 2
B_GROUP_DIM = B_WIDTH // B_GROUPS
POOL_WINDOWS = (2, 4, 8, 16)
POOL_PAD = 16

C_HEADS = 16
C_HEAD_DIM = D_MODEL // C_HEADS
C_WIDTH = C_HEADS * C_HEAD_DIM
C_SCALE = C_HEAD_DIM ** -0.5
NA_ROWS_MAX = 8
NA_COLS = 16

N_GROUPS = 4
E_PER_GROUP = 8
N_EXPERTS = N_GROUPS * E_PER_GROUP
TOP_K = 2
D_EXPERT = D_MODEL // 2

LANES = 128
ROW_TILE = 256
MOE_ROWS = 256
MOD_ROWS = 16
MOD_TILE_N = 1536
ROUTE_LANES = 128
VMEM_LIMIT_BYTES = 48 * 1024 * 1024
MASKED = -1e30

F32 = jnp.float32
BF16 = jnp.bfloat16


def _params(*semantics):
    return pltpu.CompilerParams(dimension_semantics=semantics, vmem_limit_bytes=VMEM_LIMIT_BYTES)


def _split_bf16(a):
    hi = a.astype(BF16)
    lo = (a - hi.astype(F32)).astype(BF16)
    return hi, lo


def _dot(a, b):
    return jnp.dot(a, b, preferred_element_type=F32)


def _dot_nt(a, b):
    return lax.dot_general(a, b, (((1,), (1,)), ((), ())), preferred_element_type=F32)


def _silu(a):
    return a * jax.nn.sigmoid(a)


def _mod_kernel(c_ref, w_ref, b_ref, o_ref):
    s_hi, s_lo = _split_bf16(_silu(c_ref[...]))
    w_hi, w_lo = _split_bf16(w_ref[0])
    o_ref[0] = _dot(s_hi, w_hi) + _dot(s_hi, w_lo) + _dot(s_lo, w_hi) + b_ref[0]


def _modulation(cvec, ada_w, ada_b):
    depth, d, n = ada_w.shape
    return pl.pallas_call(
        _mod_kernel,
        grid=(depth, n // MOD_TILE_N),
        in_specs=[
            pl.BlockSpec((MOD_ROWS, d), lambda l, j: (0, 0)),
            pl.BlockSpec((1, d, MOD_TILE_N), lambda l, j: (l, 0, j)),
            pl.BlockSpec((1, 1, MOD_TILE_N), lambda l, j: (l, 0, j)),
        ],
        out_specs=pl.BlockSpec((1, MOD_ROWS, MOD_TILE_N), lambda l, j: (l, 0, j)),
        out_shape=jax.ShapeDtypeStruct((depth, MOD_ROWS, n), F32),
        compiler_params=_params("parallel", "parallel"),
        name="adaln_mod",
    )(cvec, ada_w, ada_b.reshape(depth, 1, n))


def _mod_spec(chunk, blocks_per_batch, ctx_row):
    def index(i):
        return (jnp.where(i % blocks_per_batch == 0, ctx_row, i // blocks_per_batch), 0, chunk)
    return pl.BlockSpec((1, 1, D_MODEL), index)


def _norm_modulate(x, g, shift, scale):
    ms = jnp.mean(x * x, axis=-1, keepdims=True)
    return (x * lax.rsqrt(ms + EPS) * g) * (1.0 + scale) + shift


def _qk_norm_rope(a, gmat_ref, perm_ref, cos_ref, sin_ref):
    w = a.shape[-1]
    sq_hi, sq_lo = _split_bf16(a * a)
    gmat = gmat_ref[:w, :w]
    msq = _dot(sq_hi, gmat) + _dot(sq_lo, gmat)
    rot = _dot(a.astype(BF16), perm_ref[:w, :w])
    return lax.rsqrt(msq + EPS) * (a * cos_ref[...] + rot * sin_ref[...])


def _in_even_kernel(x_ref, g_ref, sh_ref, sc_ref, w_ref, cq_ref, sq_ref, ck_ref, sk_ref, gmat_ref, perm_ref,
                    q_ref, k_ref, v_ref, u_ref):
    hb = _norm_modulate(x_ref[...], g_ref[...], sh_ref[0], sc_ref[0]).astype(BF16)
    c0, c1, c2 = A_WIDTH, A_WIDTH + A_KV_WIDTH, A_WIDTH + 2 * A_KV_WIDTH
    q = _dot(hb, w_ref[:, :c0])
    q_ref[...] = _qk_norm_rope(q, gmat_ref, perm_ref, cq_ref, sq_ref).astype(BF16)
    k = _dot(hb, w_ref[:, c0:c1])
    k_ref[...] = _qk_norm_rope(k, gmat_ref, perm_ref, ck_ref, sk_ref).astype(BF16)
    v_ref[...] = _dot(hb, w_ref[:, c1:c2]).astype(BF16)
    u_ref[...] = _dot(hb, w_ref[:, c2:])


def _rope_tables(q_gain, k_gain):
    t = np.arange(SEQ)
    pos = np.stack([t // GRID_W, t % GRID_W], axis=-1).astype(np.float32)
    inv = (ROPE_THETA ** (-np.arange(ROPE_FREQS, dtype=np.float32) / ROPE_FREQS)).astype(np.float32)
    ang = pos[:, :, None] * inv
    cos = np.concatenate([np.cos(ang), np.cos(ang)], axis=-1).reshape(SEQ, A_HEAD_DIM)
    sin = np.concatenate([-np.sin(ang), np.sin(ang)], axis=-1).reshape(SEQ, A_HEAD_DIM)
    cos = np.concatenate([np.ones((CTX_LEN, A_HEAD_DIM), np.float32), cos.astype(np.float32)], axis=0)
    sin = np.concatenate([np.zeros((CTX_LEN, A_HEAD_DIM), np.float32), sin.astype(np.float32)], axis=0)
    swap = _rope_swap()
    def tables(gain, heads, scale):
        c = jnp.asarray(cos) * (gain * scale)[None, :]
        s = jnp.asarray(sin) * (gain[swap] * scale)[None, :]
        return jnp.tile(c, (1, heads)), jnp.tile(s, (1, heads))
    cq, sq = tables(q_gain, A_HEADS, A_SCALE)
    ck, sk = tables(k_gain, A_KV_HEADS, 1.0)
    return cq, sq, ck, sk


def _rope_swap():
    j = np.arange(A_HEAD_DIM)
    half = (j // ROPE_FREQS) % 2
    return np.where(half == 0, j + ROPE_FREQS, j - ROPE_FREQS)


def _head_matrices():
    lane = np.arange(A_WIDTH)
    same_head = (lane[:, None] // A_HEAD_DIM) == (lane[None, :] // A_HEAD_DIM)
    gmat = same_head.astype(np.float32) / A_HEAD_DIM
    swap = np.tile(_rope_swap(), A_HEADS) + (lane // A_HEAD_DIM) * A_HEAD_DIM
    perm = np.zeros((A_WIDTH, A_WIDTH), np.float32)
    perm[swap, lane] = 1.0
    return jnp.asarray(gmat, BF16), jnp.asarray(perm, BF16)


def _in_even(xs, mod_l, norm_g, w_in, q_gain, k_gain):
    t_rows = xs.shape[0]
    bpb = (CTX_LEN + SEQ) // ROW_TILE
    cq, sq, ck, sk = _rope_tables(q_gain, k_gain)
    gmat, perm = _head_matrices()
    n_in = w_in.shape[1]
    row = lambda w: pl.BlockSpec((ROW_TILE, w), lambda i: (i, 0))
    tab = lambda w: pl.BlockSpec((ROW_TILE, w), lambda i: (i % bpb, 0))
    full = lambda a, b: pl.BlockSpec((a, b), lambda i: (0, 0))
    return pl.pallas_call(
        _in_even_kernel,
        grid=(t_rows // ROW_TILE,),
        in_specs=[
            row(D_MODEL), full(1, D_MODEL), _mod_spec(0, bpb, BATCH), _mod_spec(1, bpb, BATCH),
            full(D_MODEL, n_in), tab(A_WIDTH), tab(A_WIDTH), tab(A_KV_WIDTH), tab(A_KV_WIDTH),
            full(A_WIDTH, A_WIDTH), full(A_WIDTH, A_WIDTH),
        ],
        out_specs=[row(A_WIDTH), row(A_KV_WIDTH), row(A_KV_WIDTH), row(B_WIDTH)],
        out_shape=[
            jax.ShapeDtypeStruct((t_rows, A_WIDTH), BF16),
            jax.ShapeDtypeStruct((t_rows, A_KV_WIDTH), BF16),
            jax.ShapeDtypeStruct((t_rows, A_KV_WIDTH), BF16),
            jax.ShapeDtypeStruct((t_rows, B_WIDTH), F32),
        ],
        compiler_params=_params("parallel"),
        name="even_in_proj",
    )(xs, norm_g.reshape(1, D_MODEL), mod_l, mod_l, w_in.astype(BF16), cq, sq, ck, sk, gmat, perm)


def _gqa_kernel(q_ref, k_ref, v_ref, o_ref):
    def attend(n_keys):
        k = k_ref[0, 0, :n_keys, :]
        v = v_ref[0, 0, :n_keys, :]
        outs = []
        for h in range(2):
            s = _dot_nt(q_ref[:, h * A_HEAD_DIM:(h + 1) * A_HEAD_DIM], k)
            m = jnp.max(s, axis=-1, keepdims=True)
            p = jnp.exp(s - m).astype(BF16)
            acc = _dot(p, v)
            outs.append(acc[:, :A_HEAD_DIM] / acc[:, A_HEAD_DIM:A_HEAD_DIM + 1])
        o_ref[...] = jnp.concatenate(outs, axis=-1).astype(BF16)

    is_ctx = pl.program_id(1) == 0

    @pl.when(is_ctx)
    def _():
        attend(CTX_LEN)

    @pl.when(jnp.logical_not(is_ctx))
    def _():
        attend(CTX_LEN + SEQ)


def _gqa(q, k, v):
    t_rows = q.shape[0]
    r = CTX_LEN + SEQ
    bpb = r // ROW_TILE
    group = A_HEADS // A_KV_HEADS
    kh = k.reshape(BATCH, r, A_KV_HEADS, A_HEAD_DIM).transpose(0, 2, 1, 3)
    vh = v.reshape(BATCH, r, A_KV_HEADS, A_HEAD_DIM).transpose(0, 2, 1, 3)
    ones = jnp.ones((BATCH, A_KV_HEADS, r, 1), BF16)
    zeros = jnp.zeros((BATCH, A_KV_HEADS, r, LANES - A_HEAD_DIM - 1), BF16)
    vext = jnp.concatenate([vh, ones, zeros], axis=-1)
    pairs = A_HEADS // 2
    return pl.pallas_call(
        _gqa_kernel,
        grid=(BATCH, bpb, pairs),
        in_specs=[
            pl.BlockSpec((ROW_TILE, LANES), lambda b, j, p: (b * bpb + j, p)),
            pl.BlockSpec((1, 1, r, A_HEAD_DIM), lambda b, j, p: (b, (2 * p) // group, 0, 0)),
            pl.BlockSpec((1, 1, r, LANES), lambda b, j, p: (b, (2 * p) // group, 0, 0)),
        ],
        out_specs=pl.BlockSpec((ROW_TILE, LANES), lambda b, j, p: (b * bpb + j, p)),
        out_shape=jax.ShapeDtypeStruct((t_rows, A_WIDTH), BF16),
        compiler_params=_params("parallel", "parallel", "parallel"),
        name="gqa_attention",
    )(q, kh, vext)


def _pool_kernel(u_ref, w_ref, s_ref, o_ref, pad_ref):
    g = pl.program_id(1)
    w_bf = w_ref[0].astype(BF16)
    for start, length in ((0, CTX_LEN), (CTX_LEN, SEQ)):
        u = u_ref[start:start + length, :]
        pad_ref[0:POOL_PAD, :] = jnp.zeros((POOL_PAD, LANES), F32)
        pad_ref[POOL_PAD:POOL_PAD + length, :] = u
        pad_ref[POOL_PAD + length:2 * POOL_PAD + length, :] = jnp.zeros((POOL_PAD, LANES), F32)
        t = lax.broadcasted_iota(jnp.int32, (length, LANES), 0)
        for gi, win in enumerate(POOL_WINDOWS):
            @pl.when(g == gi)
            def _(win=win, u=u, t=t, start=start, length=length):
                base = POOL_PAD - win // 2
                acc = pad_ref[base:base + length, :]
                for j in range(1, win):
                    acc = acc + pad_ref[base + j:base + j + length, :]
                lo = jnp.clip(t - win // 2, 0, length)
                hi = jnp.clip(t - win // 2 + win, 0, length)
                pooled = acc / (hi - lo).astype(F32) - u
                mixed = _dot(pooled.astype(BF16), w_bf) * s_ref[0]
                o_ref[start:start + length, :] = mixed.astype(BF16)


def _pool(u, pool_w, pool_scale):
    t_rows = u.shape[0]
    r = CTX_LEN + SEQ
    return pl.pallas_call(
        _pool_kernel,
        grid=(BATCH, B_GROUPS),
        in_specs=[
            pl.BlockSpec((r, B_GROUP_DIM), lambda b, g: (b, g)),
            pl.BlockSpec((1, B_GROUP_DIM, B_GROUP_DIM), lambda b, g: (g, 0, 0)),
            pl.BlockSpec((1, 1, B_GROUP_DIM), lambda b, g: (g, 0, 0)),
        ],
        out_specs=pl.BlockSpec((r, B_GROUP_DIM), lambda b, g: (b, g)),
        out_shape=jax.ShapeDtypeStruct((t_rows, B_WIDTH), BF16),
        scratch_shapes=[pltpu.VMEM((SEQ + 2 * POOL_PAD, B_GROUP_DIM), F32)],
        compiler_params=_params("parallel", "parallel"),
        name="pool_mixer",
    )(u, pool_w, pool_scale.reshape(B_GROUPS, 1, B_GROUP_DIM))


def _in_odd_kernel(x_ref, g_ref, sh_ref, sc_ref, w_ref, q_ref, k_ref, v_ref):
    hb = _norm_modulate(x_ref[...], g_ref[...], sh_ref[0], sc_ref[0]).astype(BF16)
    q_ref[...] = (_dot(hb, w_ref[:, :C_WIDTH]) * C_SCALE).astype(BF16)
    k_ref[...] = _dot(hb, w_ref[:, C_WIDTH:2 * C_WIDTH]).astype(BF16)
    v_ref[...] = _dot(hb, w_ref[:, 2 * C_WIDTH:]).astype(BF16)


def _in_odd(xs, mod_l, norm_g, w_in):
    t_rows = xs.shape[0]
    bpb = (CTX_LEN + SEQ) // ROW_TILE
    row = lambda w: pl.BlockSpec((ROW_TILE, w), lambda i: (i, 0))
    full = lambda a, b: pl.BlockSpec((a, b), lambda i: (0, 0))
    return pl.pallas_call(
        _in_odd_kernel,
        grid=(t_rows // ROW_TILE,),
        in_specs=[row(D_MODEL), full(1, D_MODEL), _mod_spec(0, bpb, BATCH), _mod_spec(1, bpb, BATCH),
                  full(D_MODEL, 3 * C_WIDTH)],
        out_specs=[row(C_WIDTH)] * 3,
        out_shape=[jax.ShapeDtypeStruct((t_rows, C_WIDTH), BF16)] * 3,
        compiler_params=_params("parallel"),
        name="odd_in_proj",
    )(xs, norm_g.reshape(1, D_MODEL), mod_l, mod_l, w_in.astype(BF16))


def _na_bias_table(rel_bias):
    kh = min(NA_ROWS_MAX, SEQ // GRID_W)
    cq = np.arange(GRID_W)
    col_start = np.clip(cq - NA_COLS // 2, 0, GRID_W - NA_COLS)
    ck = np.arange(GRID_W)
    inside = (ck[None, :] >= col_start[:, None]) & (ck[None, :] < col_start[:, None] + NA_COLS)
    col_off = np.clip(ck[None, :] - cq[:, None] + (NA_COLS - 1), 0, 2 * NA_COLS - 2)
    dr = np.arange(kh)
    row_off = np.clip(np.arange(kh)[None, :] - dr[:, None] + (NA_ROWS_MAX - 1), 0, 2 * NA_ROWS_MAX - 2)
    tab = rel_bias[:, row_off[:, None, :, None], col_off[None, :, None, :]]
    tab = jnp.where(jnp.asarray(inside)[None, None, :, None, :], tab.astype(F32), MASKED)
    return tab.reshape(C_HEADS, kh, GRID_W, kh * GRID_W)


def _na_kernel(q_ref, k_ref, v_ref, bias_ref, o_ref):
    rows = SEQ // GRID_W
    kh = min(NA_ROWS_MAX, rows)
    win = kh * GRID_W
    o_ref[0:CTX_LEN, :] = jnp.zeros((CTX_LEN, LANES), BF16)
    kc = k_ref[0:CTX_LEN, :]
    vc = v_ref[0:CTX_LEN, :]

    def one_row(r, carry):
        rs = jnp.clip(r - kh // 2, 0, rows - kh)
        dr = r - rs
        q_off = pl.multiple_of(CTX_LEN + r * GRID_W, GRID_W)
        k_off = pl.multiple_of(CTX_LEN + rs * GRID_W, GRID_W)
        qr = q_ref[pl.ds(q_off, GRID_W), :]
        kw = k_ref[pl.ds(k_off, win), :]
        vw = v_ref[pl.ds(k_off, win), :]
        outs = []
        for h in range(2):
            sl = slice(h * C_HEAD_DIM, (h + 1) * C_HEAD_DIM)
            qh = qr[:, sl]
            s_loc = _dot_nt(qh, kw[:, sl]) + bias_ref[h, dr]
            s_ctx = _dot_nt(qh, kc[:, sl])
            m = jnp.maximum(jnp.max(s_loc, axis=-1, keepdims=True), jnp.max(s_ctx, axis=-1, keepdims=True))
            p_loc = jnp.exp(s_loc - m)
            p_ctx = jnp.exp(s_ctx - m)
            denom = jnp.sum(p_loc, axis=-1, keepdims=True) + jnp.sum(p_ctx, axis=-1, keepdims=True)
            o = _dot(p_loc.astype(BF16), vw[:, sl]) + _dot(p_ctx.astype(BF16), vc[:, sl])
            outs.append(o / denom)
        o_ref[pl.ds(q_off, GRID_W), :] = jnp.concatenate(outs, axis=-1).astype(BF16)
        return carry

    lax.fori_loop(0, rows, one_row, 0)


def _na(q, k, v, rel_bias):
    t_rows = q.shape[0]
    r = CTX_LEN + SEQ
    kh = min(NA_ROWS_MAX, SEQ // GRID_W)
    bias = _na_bias_table(rel_bias)
    blk = pl.BlockSpec((r, LANES), lambda b, p: (b, p))
    return pl.pallas_call(
        _na_kernel,
        grid=(BATCH, C_HEADS // 2),
        in_specs=[blk, blk, blk,
                  pl.BlockSpec((2, kh, GRID_W, kh * GRID_W), lambda b, p: (p, 0, 0, 0))],
        out_specs=blk,
        out_shape=jax.ShapeDtypeStruct((t_rows, C_WIDTH), BF16),
        compiler_params=_params("parallel", "parallel"),
        name="neighbourhood_attention",
    )(q, k, v, bias)


def _out_kernel(n_lat, *refs):
    lat_refs = refs[:n_lat]
    w_ref, x_ref, g1_ref, g_ref, sh_ref, sc_ref, wr_ref, br_ref, xo_ref, h_ref, lg_ref = refs[n_lat:]
    y = None
    off = 0
    for lat in lat_refs:
        width = lat.shape[-1]
        part = _dot(lat[...], w_ref[off:off + width, :])
        y = part if y is None else y + part
        off += width
    x_new = x_ref[...] + g1_ref[0] * y
    xo_ref[...] = x_new
    h = _norm_modulate(x_new, g_ref[...], sh_ref[0], sc_ref[0])
    h_ref[...] = h
    h_hi, h_lo = _split_bf16(h)
    both = _dot(h_hi, wr_ref[...])
    lg_ref[...] = (both[:, :ROUTE_LANES] + both[:, ROUTE_LANES:]
                   + _dot(h_lo, wr_ref[:, :ROUTE_LANES]) + br_ref[...])


def _router_weights(wg, bg, we, be):
    w = jnp.concatenate([wg, we], axis=1).astype(F32)
    w = jnp.pad(w, ((0, 0), (0, ROUTE_LANES - w.shape[1])))
    w_hi, w_lo = _split_bf16(w)
    b = jnp.pad(jnp.concatenate([bg, be]).astype(F32), (0, ROUTE_LANES - N_GROUPS - N_EXPERTS))
    return jnp.concatenate([w_hi, w_lo], axis=1), b.reshape(1, ROUTE_LANES)


def _out_proj(lats, w_out, xs, mod_l, norm_g, wr, br):
    t_rows = xs.shape[0]
    bpb = (CTX_LEN + SEQ) // ROW_TILE
    row = lambda w: pl.BlockSpec((ROW_TILE, w), lambda i: (i, 0))
    full = lambda a, b: pl.BlockSpec((a, b), lambda i: (0, 0))
    return pl.pallas_call(
        functools.partial(_out_kernel, len(lats)),
        grid=(t_rows // ROW_TILE,),
        in_specs=[row(l.shape[1]) for l in lats] + [
            full(*w_out.shape), row(D_MODEL), _mod_spec(2, bpb, BATCH), full(1, D_MODEL),
            _mod_spec(3, bpb, BATCH), _mod_spec(4, bpb, BATCH), full(D_MODEL, 2 * ROUTE_LANES), full(1, ROUTE_LANES),
        ],
        out_specs=[row(D_MODEL), row(D_MODEL), row(ROUTE_LANES)],
        out_shape=[
            jax.ShapeDtypeStruct((t_rows, D_MODEL), F32),
            jax.ShapeDtypeStruct((t_rows, D_MODEL), F32),
            jax.ShapeDtypeStruct((t_rows, ROUTE_LANES), F32),
        ],
        compiler_params=_params("parallel"),
        name="out_proj_ffn_norm",
    )(*lats, w_out.astype(BF16), xs, mod_l, norm_g.reshape(1, D_MODEL), mod_l, mod_l, wr, br)


def _route(logits):
    t_rows = logits.shape[0]
    g_logits = logits[:, :N_GROUPS]
    e_logits = logits[:, N_GROUPS:N_GROUPS + N_EXPERTS].reshape(t_rows, N_GROUPS, E_PER_GROUP)
    g_idx = jnp.argmax(g_logits, axis=-1)
    g_w = jnp.take_along_axis(jax.nn.softmax(g_logits, axis=-1), g_idx[:, None], axis=1)[:, 0]
    e_sel = jnp.take_along_axis(e_logits, g_idx[:, None, None], axis=1)[:, 0]
    top_p, top_e = lax.top_k(jax.nn.softmax(e_sel, axis=-1), TOP_K)
    top_p = top_p / jnp.sum(top_p, axis=-1, keepdims=True)
    weights = g_w[:, None] * top_p
    flat_e = (g_idx[:, None] * E_PER_GROUP + top_e).astype(jnp.int32).reshape(-1)
    onehot = (flat_e[:, None] == jnp.arange(N_EXPERTS, dtype=jnp.int32)[None, :]).astype(jnp.int32)
    csum = jnp.cumsum(onehot, axis=0)
    rank = jnp.take_along_axis(csum, flat_e[:, None], axis=1)[:, 0] - 1
    counts = csum[-1]
    n_blk = (counts + MOE_ROWS - 1) // MOE_ROWS
    b_end = jnp.cumsum(n_blk)
    b_start = b_end - n_blk
    dest = (b_start[flat_e] * MOE_ROWS + rank).astype(jnp.int32).reshape(t_rows, TOP_K)
    n_used = b_end[-1].astype(jnp.int32)
    g_blocks = _moe_blocks(t_rows)
    blk = jnp.minimum(jnp.arange(g_blocks, dtype=jnp.int32), n_used - 1)
    block_e = jnp.minimum(jnp.searchsorted(b_end, blk, side='right'), N_EXPERTS - 1).astype(jnp.int32)
    return dest, weights.astype(F32), block_e, n_used.reshape(1)


def _moe_blocks(t_rows):
    return (t_rows * TOP_K) // MOE_ROWS + N_EXPERTS


def _slot_major(dest):
    t_rows = dest.shape[0]
    return dest.reshape(t_rows // ROW_TILE, ROW_TILE, TOP_K).transpose(0, 2, 1).reshape(
        t_rows // ROW_TILE, 1, TOP_K * ROW_TILE)


def _row_copy(src_ref, src_row, dst_ref, dst_row, sem):
    return pltpu.make_async_copy(src_ref.at[pl.ds(src_row, 1)], dst_ref.at[pl.ds(dst_row, 1)], sem)


def _dispatch_kernel(dest_ref, h_ref, init_ref, xpad_ref, sem):
    del init_ref

    def copies(r):
        return [_row_copy(h_ref, r, xpad_ref, dest_ref[0, 0, k * ROW_TILE + r], sem) for k in range(TOP_K)]

    def start(r, c):
        for cp in copies(r):
            cp.start()
        return c

    def wait(r, c):
        for cp in copies(r):
            cp.wait()
        return c

    lax.fori_loop(0, ROW_TILE, start, 0)
    lax.fori_loop(0, ROW_TILE, wait, 0)


def _dispatch(h, dest_blocks):
    t_rows = h.shape[0]
    n_pad = _moe_blocks(t_rows) * MOE_ROWS
    init = jnp.zeros((n_pad, D_MODEL), F32)
    return pl.pallas_call(
        _dispatch_kernel,
        grid=(t_rows // ROW_TILE,),
        in_specs=[
            pl.BlockSpec((1, 1, TOP_K * ROW_TILE), lambda i: (i, 0, 0), memory_space=pltpu.SMEM),
            pl.BlockSpec((ROW_TILE, D_MODEL), lambda i: (i, 0)),
            pl.BlockSpec(memory_space=pl.ANY),
        ],
        out_specs=pl.BlockSpec(memory_space=pl.ANY),
        out_shape=jax.ShapeDtypeStruct((n_pad, D_MODEL), F32),
        scratch_shapes=[pltpu.SemaphoreType.DMA(())],
        input_output_aliases={2: 0},
        compiler_params=_params("arbitrary"),
        name="moe_dispatch",
    )(dest_blocks, h, init)


def _expert_kernel(be_ref, nu_ref, x_ref, w1_ref, w3_ref, w2_ref, y_ref, w1_bf, w3_bf, w2_bf):
    i = pl.program_id(0)
    fresh = jnp.logical_or(i == 0, be_ref[i] != be_ref[jnp.maximum(i - 1, 0)])

    @pl.when(fresh)
    def _():
        w1_bf[...] = w1_ref[0, 0].astype(BF16)
        w3_bf[...] = w3_ref[0, 0].astype(BF16)
        w2_bf[...] = w2_ref[0, 0].astype(BF16)

    used = i < nu_ref[0]

    @pl.when(used)
    def _():
        xb = x_ref[...].astype(BF16)
        hid = _silu(_dot(xb, w1_bf[...])) * _dot(xb, w3_bf[...])
        y_ref[...] = _dot(hid.astype(BF16), w2_bf[...])

    @pl.when(jnp.logical_not(used))
    def _():
        y_ref[...] = jnp.zeros(y_ref.shape, F32)


def _experts(xpad, block_e, n_used, w1, w3, w2, layer):
    g_blocks = xpad.shape[0] // MOE_ROWS
    x_index = lambda i, be, nu: (jnp.minimum(i, nu[0] - 1), 0)
    w_index = lambda i, be, nu: (layer, be[i], 0, 0)
    return pl.pallas_call(
        _expert_kernel,
        grid_spec=pltpu.PrefetchScalarGridSpec(
            num_scalar_prefetch=2,
            grid=(g_blocks,),
            in_specs=[
                pl.BlockSpec((MOE_ROWS, D_MODEL), x_index),
                pl.BlockSpec((1, 1, D_MODEL, D_EXPERT), w_index),
                pl.BlockSpec((1, 1, D_MODEL, D_EXPERT), w_index),
                pl.BlockSpec((1, 1, D_EXPERT, D_MODEL), w_index),
            ],
            out_specs=pl.BlockSpec((MOE_ROWS, D_MODEL), lambda i, be, nu: (i, 0)),
            scratch_shapes=[
                pltpu.VMEM((D_MODEL, D_EXPERT), BF16),
                pltpu.VMEM((D_MODEL, D_EXPERT), BF16),
                pltpu.VMEM((D_EXPERT, D_MODEL), BF16),
            ],
        ),
        out_shape=jax.ShapeDtypeStruct((g_blocks * MOE_ROWS, D_MODEL), F32),
        compiler_params=_params("arbitrary"),
        name="moe_experts",
    )(block_e, n_used, xpad, w1, w3, w2)


def _combine_kernel(final, dest_ref, x_ref, wt_ref, g2_ref, fg_ref, ypad_ref, o_ref, ybuf, sem):
    def copies(r):
        return [_row_copy(ypad_ref, dest_ref[0, 0, k * ROW_TILE + r], ybuf.at[k], r, sem) for k in range(TOP_K)]

    def start(r, c):
        for cp in copies(r):
            cp.start()
        return c

    def wait(r, c):
        for cp in copies(r):
            cp.wait()
        return c

    lax.fori_loop(0, ROW_TILE, start, 0)
    lax.fori_loop(0, ROW_TILE, wait, 0)
    wt = wt_ref[...]
    f = wt[:, 0:1] * ybuf[0] + wt[:, 1:2] * ybuf[1]
    x_new = x_ref[...] + g2_ref[0] * f
    if final:
        ms = jnp.mean(x_new * x_new, axis=-1, keepdims=True)
        x_new = x_new * lax.rsqrt(ms + EPS) * fg_ref[...]
    o_ref[...] = x_new


def _combine(ypad, dest_blocks, weights, xs, mod_l, final_g, final):
    t_rows = xs.shape[0]
    bpb = (CTX_LEN + SEQ) // ROW_TILE
    if final:
        xpb = SEQ // ROW_TILE
        n_out = BATCH * xpb
        blk = lambda i: (i // xpb) * bpb + i % xpb + 1
        g2_index = lambda i: (i // xpb, 0, 5)
    else:
        n_out = t_rows // ROW_TILE
        blk = lambda i: i
        g2_index = lambda i: (jnp.where(i % bpb == 0, BATCH, i // bpb), 0, 5)
    return pl.pallas_call(
        functools.partial(_combine_kernel, final),
        grid=(n_out,),
        in_specs=[
            pl.BlockSpec((1, 1, TOP_K * ROW_TILE), lambda i: (blk(i), 0, 0), memory_space=pltpu.SMEM),
            pl.BlockSpec((ROW_TILE, D_MODEL), lambda i: (blk(i), 0)),
            pl.BlockSpec((ROW_TILE, TOP_K), lambda i: (blk(i), 0)),
            pl.BlockSpec((1, 1, D_MODEL), g2_index),
            pl.BlockSpec((1, D_MODEL), lambda i: (0, 0)),
            pl.BlockSpec(memory_space=pl.ANY),
        ],
        out_specs=pl.BlockSpec((ROW_TILE, D_MODEL), lambda i: (i, 0)),
        out_shape=jax.ShapeDtypeStruct((n_out * ROW_TILE, D_MODEL), F32),
        scratch_shapes=[pltpu.VMEM((TOP_K, ROW_TILE, D_MODEL), F32), pltpu.SemaphoreType.DMA(())],
        compiler_params=_params("arbitrary"),
        name="moe_combine_final" if final else "moe_combine",
    )(dest_blocks, xs, weights, mod_l, final_g.reshape(1, D_MODEL), ypad)


def _moe(h, logits, xs, mod_l, w1, w3, w2, layer, final_g, final):
    dest, weights, block_e, n_used = _route(logits)
    dest_blocks = _slot_major(dest)
    xpad = _dispatch(h, dest_blocks)
    ypad = _experts(xpad, block_e, n_used, w1, w3, w2, layer)
    return _combine(ypad, dest_blocks, weights, xs, mod_l, final_g, final)


def kernel(x, c, ctx, c_ctx, ada_w, ada_b, norm_mix_g, norm_ffn_g, even_w_in, even_w_out, a_q_gain, a_k_gain, pool_w, pool_scale, odd_w_in, odd_w_out, na_rel_bias, moe_w_group, moe_b_group, moe_w_expert, moe_b_expert, moe_w1, moe_w3, moe_w2, final_g):
    r = CTX_LEN + SEQ
    xs = jnp.concatenate([ctx, x], axis=1).reshape(BATCH * r, D_MODEL)
    cvec = jnp.concatenate([c, c_ctx[None, :], jnp.zeros((MOD_ROWS - BATCH - 1, D_MODEL), F32)], axis=0)
    mod = _modulation(cvec, ada_w, ada_b)
    for l in range(DEPTH):
        mod_l = mod[l].reshape(MOD_ROWS, 1, N_MOD * D_MODEL)
        i = l // 2
        if l % 2 == 0:
            q, k, v, u = _in_even(xs, mod_l, norm_mix_g[l], even_w_in[i], a_q_gain[i], a_k_gain[i])
            lats = [_gqa(q, k, v), _pool(u, pool_w[i], pool_scale[i])]
            w_out = even_w_out[i]
        else:
            q, k, v = _in_odd(xs, mod_l, norm_mix_g[l], odd_w_in[i])
            lats = [_na(q, k, v, na_rel_bias[i])]
            w_out = odd_w_out[i]
        wr, br = _router_weights(moe_w_group[l], moe_b_group[l], moe_w_expert[l], moe_b_expert[l])
        xs, h, logits = _out_proj(lats, w_out, xs, mod_l, norm_ffn_g[l], wr, br)
        xs = _moe(h, logits, xs, mod_l, moe_w1, moe_w3, moe_w2, l, final_g, l == DEPTH - 1)
    return xs.reshape(BATCH, SEQ, D_MODEL)
```

```python
import functools

import jax
import jax.numpy as jnp
import numpy as np
from jax import lax
from jax.experimental import pallas as pl
from jax.experimental.pallas import tpu as pltpu

D_MODEL = 1024
BATCH = 8
SEQ = 4096
DEPTH = 2
GRID_W = 64
CTX_LEN = 256
EPS = 1e-6
N_MOD = 6

A_HEADS = 8
A_KV_HEADS = 2
A_HEAD_DIM = 64
A_WIDTH = A_HEADS * A_HEAD_DIM
A_KV_WIDTH = A_KV_HEADS * A_HEAD_DIM
A_SCALE = A_HEAD_DIM ** -0.5
ROPE_THETA = 10000.0
ROPE_FREQS = A_HEAD_DIM // 4

B_GROUPS = 4
B_WIDTH = D_MODEL // 2
B_GROUP_DIM = B_WIDTH // B_GROUPS
POOL_WINDOWS = (2, 4, 8, 16)
POOL_PAD = 16

C_HEADS = 16
C_HEAD_DIM = D_MODEL // C_HEADS
C_WIDTH = C_HEADS * C_HEAD_DIM
C_SCALE = C_HEAD_DIM ** -0.5
NA_ROWS_MAX = 8
NA_COLS = 16
NA_BAND = 4
LOG2_E = 1.4426950408889634

N_GROUPS = 4
E_PER_GROUP = 8
N_EXPERTS = N_GROUPS * E_PER_GROUP
TOP_K = 2
D_EXPERT = D_MODEL // 2

LANES = 128
TOKEN_SUBLANES = D_MODEL // LANES
ROW_TILE = 256
MOE_ROWS = 256
DMA_UNROLL = 8
MOD_ROWS = 16
MOD_TILE_N = 1536
ROUTE_LANES = 128
VMEM_LIMIT_BYTES = 48 * 1024 * 1024
MASKED = -1e30

F32 = jnp.float32
BF16 = jnp.bfloat16


def _params(*semantics):
    return pltpu.CompilerParams(dimension_semantics=semantics, vmem_limit_bytes=VMEM_LIMIT_BYTES)


def _split_bf16(a):
    hi = a.astype(BF16)
    lo = (a - hi.astype(F32)).astype(BF16)
    return hi, lo


def _dot(a, b):
    return jnp.dot(a, b, preferred_element_type=F32)


def _dot_nt(a, b):
    return lax.dot_general(a, b, (((1,), (1,)), ((), ())), preferred_element_type=F32)


def _silu(a):
    return a * jax.nn.sigmoid(a)


def _mod_kernel(c_ref, w_ref, b_ref, o_ref):
    s_hi, s_lo = _split_bf16(_silu(c_ref[...]))
    w_hi, w_lo = _split_bf16(w_ref[0])
    o_ref[0] = _dot(s_hi, w_hi) + _dot(s_hi, w_lo) + _dot(s_lo, w_hi) + b_ref[0]


def _modulation(cvec, ada_w, ada_b):
    depth, d, n = ada_w.shape
    return pl.pallas_call(
        _mod_kernel,
        grid=(depth, n // MOD_TILE_N),
        in_specs=[
            pl.BlockSpec((MOD_ROWS, d), lambda l, j: (0, 0)),
            pl.BlockSpec((1, d, MOD_TILE_N), lambda l, j: (l, 0, j)),
            pl.BlockSpec((1, 1, MOD_TILE_N), lambda l, j: (l, 0, j)),
        ],
        out_specs=pl.BlockSpec((1, MOD_ROWS, MOD_TILE_N), lambda l, j: (l, 0, j)),
        out_shape=jax.ShapeDtypeStruct((depth, MOD_ROWS, n), F32),
        compiler_params=_params("parallel", "parallel"),
        name="adaln_mod",
    )(cvec, ada_w, ada_b.reshape(depth, 1, n))


def _mod_spec(chunk, blocks_per_batch, ctx_row):
    def index(i):
        return (jnp.where(i % blocks_per_batch == 0, ctx_row, i // blocks_per_batch), 0, chunk)
    return pl.BlockSpec((1, 1, D_MODEL), index)


def _store_token_tiles(ref, rows):
    n = rows.shape[0]
    for s in range(TOKEN_SUBLANES):
        ref[pl.ds(s, n, stride=TOKEN_SUBLANES), :] = rows[:, s * LANES:(s + 1) * LANES]


def _load_token_tiles(ref, n, s):
    return ref[pl.ds(s, n, stride=TOKEN_SUBLANES), :]


def _norm_modulate(x, g, shift, scale):
    ms = jnp.mean(x * x, axis=-1, keepdims=True)
    return (x * lax.rsqrt(ms + EPS) * g) * (1.0 + scale) + shift


def _qk_norm_rope(a, gmat_ref, perm_ref, cos_ref, sin_ref):
    w = a.shape[-1]
    sq_hi, sq_lo = _split_bf16(a * a)
    gmat = gmat_ref[:w, :w]
    msq = _dot(sq_hi, gmat) + _dot(sq_lo, gmat)
    rot = _dot(a.astype(BF16), perm_ref[:w, :w])
    return lax.rsqrt(msq + EPS) * (a * cos_ref[...] + rot * sin_ref[...])


def _in_even_kernel(x_ref, g_ref, sh_ref, sc_ref, w_ref, cq_ref, sq_ref, ck_ref, sk_ref, gmat_ref, perm_ref,
                    q_ref, k_ref, v_ref, u_ref):
    hb = _norm_modulate(x_ref[...], g_ref[...], sh_ref[0], sc_ref[0]).astype(BF16)
    c0, c1, c2 = A_WIDTH, A_WIDTH + A_KV_WIDTH, A_WIDTH + 2 * A_KV_WIDTH
    q = _dot(hb, w_ref[:, :c0])
    q_ref[...] = _qk_norm_rope(q, gmat_ref, perm_ref, cq_ref, sq_ref).astype(BF16)
    k = _dot(hb, w_ref[:, c0:c1])
    k_ref[...] = _qk_norm_rope(k, gmat_ref, perm_ref, ck_ref, sk_ref).astype(BF16)
    v_ref[...] = _dot(hb, w_ref[:, c1:c2]).astype(BF16)
    u_ref[...] = _dot(hb, w_ref[:, c2:])


def _rope_tables(q_gain, k_gain):
    t = np.arange(SEQ)
    pos = np.stack([t // GRID_W, t % GRID_W], axis=-1).astype(np.float32)
    inv = (ROPE_THETA ** (-np.arange(ROPE_FREQS, dtype=np.float32) / ROPE_FREQS)).astype(np.float32)
    ang = pos[:, :, None] * inv
    cos = np.concatenate([np.cos(ang), np.cos(ang)], axis=-1).reshape(SEQ, A_HEAD_DIM)
    sin = np.concatenate([-np.sin(ang), np.sin(ang)], axis=-1).reshape(SEQ, A_HEAD_DIM)
    cos = np.concatenate([np.ones((CTX_LEN, A_HEAD_DIM), np.float32), cos.astype(np.float32)], axis=0)
    sin = np.concatenate([np.zeros((CTX_LEN, A_HEAD_DIM), np.float32), sin.astype(np.float32)], axis=0)
    swap = _rope_swap()
    def tables(gain, heads, scale):
        c = jnp.asarray(cos) * (gain * scale)[None, :]
        s = jnp.asarray(sin) * (gain[swap] * scale)[None, :]
        return jnp.tile(c, (1, heads)), jnp.tile(s, (1, heads))
    cq, sq = tables(q_gain, A_HEADS, A_SCALE * LOG2_E)
    ck, sk = tables(k_gain, A_KV_HEADS, 1.0)
    return cq, sq, ck, sk


def _rope_swap():
    j = np.arange(A_HEAD_DIM)
    half = (j // ROPE_FREQS) % 2
    return np.where(half == 0, j + ROPE_FREQS, j - ROPE_FREQS)


def _head_matrices():
    lane = np.arange(A_WIDTH)
    same_head = (lane[:, None] // A_HEAD_DIM) == (lane[None, :] // A_HEAD_DIM)
    gmat = same_head.astype(np.float32) / A_HEAD_DIM
    swap = np.tile(_rope_swap(), A_HEADS) + (lane // A_HEAD_DIM) * A_HEAD_DIM
    perm = np.zeros((A_WIDTH, A_WIDTH), np.float32)
    perm[swap, lane] = 1.0
    return jnp.asarray(gmat, BF16), jnp.asarray(perm, BF16)


def _in_even(xs, mod_l, norm_g, w_in, q_gain, k_gain):
    t_rows = xs.shape[0]
    bpb = (CTX_LEN + SEQ) // ROW_TILE
    cq, sq, ck, sk = _rope_tables(q_gain, k_gain)
    gmat, perm = _head_matrices()
    n_in = w_in.shape[1]
    row = lambda w: pl.BlockSpec((ROW_TILE, w), lambda i: (i, 0))
    tab = lambda w: pl.BlockSpec((ROW_TILE, w), lambda i: (i % bpb, 0))
    full = lambda a, b: pl.BlockSpec((a, b), lambda i: (0, 0))
    return pl.pallas_call(
        _in_even_kernel,
        grid=(t_rows // ROW_TILE,),
        in_specs=[
            row(D_MODEL), full(1, D_MODEL), _mod_spec(0, bpb, BATCH), _mod_spec(1, bpb, BATCH),
            full(D_MODEL, n_in), tab(A_WIDTH), tab(A_WIDTH), tab(A_KV_WIDTH), tab(A_KV_WIDTH),
            full(A_WIDTH, A_WIDTH), full(A_WIDTH, A_WIDTH),
        ],
        out_specs=[row(A_WIDTH), row(A_KV_WIDTH), row(A_KV_WIDTH), row(B_WIDTH)],
        out_shape=[
            jax.ShapeDtypeStruct((t_rows, A_WIDTH), BF16),
            jax.ShapeDtypeStruct((t_rows, A_KV_WIDTH), BF16),
            jax.ShapeDtypeStruct((t_rows, A_KV_WIDTH), BF16),
            jax.ShapeDtypeStruct((t_rows, B_WIDTH), F32),
        ],
        compiler_params=_params("parallel"),
        name="even_in_proj",
    )(xs, norm_g.reshape(1, D_MODEL), mod_l, mod_l, w_in.astype(BF16), cq, sq, ck, sk, gmat, perm)


def _gqa_kernel(q_ref, k_ref, v_ref, o_ref):
    def attend(n_keys):
        k = k_ref[0, 0, :n_keys, :]
        v = v_ref[0, 0, :n_keys, :]
        outs = []
        for h in range(2):
            s = _dot_nt(q_ref[:, h * A_HEAD_DIM:(h + 1) * A_HEAD_DIM], k)
            m = jnp.max(s, axis=-1, keepdims=True)
            p = jnp.exp2(s - m).astype(BF16)
            acc = _dot(p, v)
            outs.append(acc[:, :A_HEAD_DIM] / acc[:, A_HEAD_DIM:A_HEAD_DIM + 1])
        o_ref[...] = jnp.concatenate(outs, axis=-1).astype(BF16)

    is_ctx = pl.program_id(1) == 0

    @pl.when(is_ctx)
    def _():
        attend(CTX_LEN)

    @pl.when(jnp.logical_not(is_ctx))
    def _():
        attend(CTX_LEN + SEQ)


def _gqa(q, k, v):
    t_rows = q.shape[0]
    r = CTX_LEN + SEQ
    bpb = r // ROW_TILE
    group = A_HEADS // A_KV_HEADS
    kh = k.reshape(BATCH, r, A_KV_HEADS, A_HEAD_DIM).transpose(0, 2, 1, 3)
    vh = v.reshape(BATCH, r, A_KV_HEADS, A_HEAD_DIM).transpose(0, 2, 1, 3)
    ones = jnp.ones((BATCH, A_KV_HEADS, r, 1), BF16)
    zeros = jnp.zeros((BATCH, A_KV_HEADS, r, LANES - A_HEAD_DIM - 1), BF16)
    vext = jnp.concatenate([vh, ones, zeros], axis=-1)
    pairs = A_HEADS // 2
    return pl.pallas_call(
        _gqa_kernel,
        grid=(BATCH, bpb, pairs),
        in_specs=[
            pl.BlockSpec((ROW_TILE, LANES), lambda b, j, p: (b * bpb + j, p)),
            pl.BlockSpec((1, 1, r, A_HEAD_DIM), lambda b, j, p: (b, (2 * p) // group, 0, 0)),
            pl.BlockSpec((1, 1, r, LANES), lambda b, j, p: (b, (2 * p) // group, 0, 0)),
        ],
        out_specs=pl.BlockSpec((ROW_TILE, LANES), lambda b, j, p: (b * bpb + j, p)),
        out_shape=jax.ShapeDtypeStruct((t_rows, A_WIDTH), BF16),
        compiler_params=_params("parallel", "parallel", "parallel"),
        name="gqa_attention",
    )(q, kh, vext)


def _pool_kernel(u_ref, w_ref, s_ref, o_ref, pad_ref):
    g = pl.program_id(1)
    w_bf = w_ref[0].astype(BF16)
    for start, length in ((0, CTX_LEN), (CTX_LEN, SEQ)):
        u = u_ref[start:start + length, :]
        pad_ref[0:POOL_PAD, :] = jnp.zeros((POOL_PAD, LANES), F32)
        pad_ref[POOL_PAD:POOL_PAD + length, :] = u
        pad_ref[POOL_PAD + length:2 * POOL_PAD + length, :] = jnp.zeros((POOL_PAD, LANES), F32)
        t = lax.broadcasted_iota(jnp.int32, (length, LANES), 0)
        for gi, win in enumerate(POOL_WINDOWS):
            @pl.when(g == gi)
            def _(win=win, u=u, t=t, start=start, length=length):
                base = POOL_PAD - win // 2
                acc = pad_ref[base:base + length, :]
                for j in range(1, win):
                    acc = acc + pad_ref[base + j:base + j + length, :]
                lo = jnp.clip(t - win // 2, 0, length)
                hi = jnp.clip(t - win // 2 + win, 0, length)
                pooled = acc / (hi - lo).astype(F32) - u
                mixed = _dot(pooled.astype(BF16), w_bf) * s_ref[0]
                o_ref[start:start + length, :] = mixed.astype(BF16)


def _pool(u, pool_w, pool_scale):
    t_rows = u.shape[0]
    r = CTX_LEN + SEQ
    return pl.pallas_call(
        _pool_kernel,
        grid=(BATCH, B_GROUPS),
        in_specs=[
            pl.BlockSpec((r, B_GROUP_DIM), lambda b, g: (b, g)),
            pl.BlockSpec((1, B_GROUP_DIM, B_GROUP_DIM), lambda b, g: (g, 0, 0)),
            pl.BlockSpec((1, 1, B_GROUP_DIM), lambda b, g: (g, 0, 0)),
        ],
        out_specs=pl.BlockSpec((r, B_GROUP_DIM), lambda b, g: (b, g)),
        out_shape=jax.ShapeDtypeStruct((t_rows, B_WIDTH), BF16),
        scratch_shapes=[pltpu.VMEM((SEQ + 2 * POOL_PAD, B_GROUP_DIM), F32)],
        compiler_params=_params("parallel", "parallel"),
        name="pool_mixer",
    )(u, pool_w, pool_scale.reshape(B_GROUPS, 1, B_GROUP_DIM))


def _in_odd_kernel(x_ref, g_ref, sh_ref, sc_ref, w_ref, q_ref, k_ref, v_ref):
    hb = _norm_modulate(x_ref[...], g_ref[...], sh_ref[0], sc_ref[0]).astype(BF16)
    q_ref[...] = (_dot(hb, w_ref[:, :C_WIDTH]) * (C_SCALE * LOG2_E)).astype(BF16)
    k_ref[...] = _dot(hb, w_ref[:, C_WIDTH:2 * C_WIDTH]).astype(BF16)
    v_ref[...] = _dot(hb, w_ref[:, 2 * C_WIDTH:]).astype(BF16)


def _in_odd(xs, mod_l, norm_g, w_in):
    t_rows = xs.shape[0]
    bpb = (CTX_LEN + SEQ) // ROW_TILE
    row = lambda w: pl.BlockSpec((ROW_TILE, w), lambda i: (i, 0))
    full = lambda a, b: pl.BlockSpec((a, b), lambda i: (0, 0))
    return pl.pallas_call(
        _in_odd_kernel,
        grid=(t_rows // ROW_TILE,),
        in_specs=[row(D_MODEL), full(1, D_MODEL), _mod_spec(0, bpb, BATCH), _mod_spec(1, bpb, BATCH),
                  full(D_MODEL, 3 * C_WIDTH)],
        out_specs=[row(C_WIDTH)] * 3,
        out_shape=[jax.ShapeDtypeStruct((t_rows, C_WIDTH), BF16)] * 3,
        compiler_params=_params("parallel"),
        name="odd_in_proj",
    )(xs, norm_g.reshape(1, D_MODEL), mod_l, mod_l, w_in.astype(BF16))


def _na_plan():
    rows = SEQ // GRID_W
    kh = min(NA_ROWS_MAX, rows)
    win_rows = min(kh + NA_BAND - 1, rows)
    starts, types, patterns = [], [], []
    for r0 in range(0, rows, NA_BAND):
        ws = int(np.clip(r0 - kh // 2, 0, rows - win_rows))
        pat = np.full((NA_BAND, win_rows), -1, np.int64)
        for rr in range(NA_BAND):
            r = r0 + rr
            rs = int(np.clip(r - kh // 2, 0, rows - kh))
            for wi in range(win_rows):
                if rs <= ws + wi < rs + kh:
                    pat[rr, wi] = ws + wi - r + (NA_ROWS_MAX - 1)
        for t, known in enumerate(patterns):
            if np.array_equal(known, pat):
                types.append(t)
                break
        else:
            types.append(len(patterns))
            patterns.append(pat)
        starts.append(ws)
    return win_rows, np.asarray(starts, np.int32), np.asarray(types, np.int32), patterns


def _na_bias_table(rel_bias, patterns):
    n_off = 2 * NA_ROWS_MAX - 1
    cq = np.arange(GRID_W)
    col_start = np.clip(cq - NA_COLS // 2, 0, GRID_W - NA_COLS)
    ck = np.arange(GRID_W)
    inside = (ck[None, :] >= col_start[:, None]) & (ck[None, :] < col_start[:, None] + NA_COLS)
    shift = GRID_W - NA_COLS
    padded = jnp.pad(rel_bias.astype(F32) * LOG2_E, ((0, 0), (0, 0), (shift, shift)))
    tcol = jnp.stack([padded[:, :, GRID_W - 1 - c:2 * GRID_W - 1 - c] for c in range(GRID_W)], axis=2)
    tcol = jnp.where(jnp.asarray(inside)[None, None], tcol, MASKED)
    tcol = jnp.concatenate([tcol, jnp.full((C_HEADS, 1, GRID_W, GRID_W), MASKED, F32)], axis=1)
    tables = []
    for pat in patterns:
        idx = np.where(pat < 0, n_off, pat)
        blocks = tcol[:, idx].transpose(0, 1, 3, 2, 4)
        tables.append(blocks.reshape(C_HEADS, idx.shape[0] * GRID_W, idx.shape[1] * GRID_W))
    return jnp.stack(tables, axis=1)


def _na_kernel(ws_ref, ty_ref, q_ref, k_ref, v_ref, bias_ref, o_ref):
    n_bands = (SEQ // GRID_W) // NA_BAND
    band_q = NA_BAND * GRID_W
    win = bias_ref.shape[-1]
    o_ref[0:CTX_LEN, :] = jnp.zeros((CTX_LEN, LANES), BF16)
    kc = k_ref[0:CTX_LEN, :]
    vc = v_ref[0:CTX_LEN, :]
    first = lax.broadcasted_iota(jnp.int32, (1, LANES), 1) < C_HEAD_DIM

    def one_band(b, carry):
        q_off = pl.multiple_of(CTX_LEN + b * band_q, GRID_W)
        k_off = pl.multiple_of(CTX_LEN + ws_ref[b] * GRID_W, GRID_W)
        qb = q_ref[pl.ds(q_off, band_q), :]
        kw = k_ref[pl.ds(k_off, win), :]
        vw = v_ref[pl.ds(k_off, win), :]
        outs = []
        for h in range(2):
            qm = jnp.where(first if h == 0 else jnp.logical_not(first), qb, jnp.zeros_like(qb))
            s_loc = _dot_nt(qm, kw) + bias_ref[h, ty_ref[b]]
            s_ctx = _dot_nt(qm, kc)
            m = jnp.maximum(jnp.max(s_loc, axis=-1, keepdims=True), jnp.max(s_ctx, axis=-1, keepdims=True))
            p_loc = jnp.exp2(s_loc - m)
            p_ctx = jnp.exp2(s_ctx - m)
            denom = jnp.sum(p_loc, axis=-1, keepdims=True) + jnp.sum(p_ctx, axis=-1, keepdims=True)
            o = _dot(p_loc.astype(BF16), vw) + _dot(p_ctx.astype(BF16), vc)
            outs.append(o / denom)
        o_ref[pl.ds(q_off, band_q), :] = jnp.where(first, outs[0], outs[1]).astype(BF16)
        return carry

    lax.fori_loop(0, n_bands, one_band, 0)


def _na(q, k, v, rel_bias):
    t_rows = q.shape[0]
    r = CTX_LEN + SEQ
    win_rows, starts, types, patterns = _na_plan()
    bias = _na_bias_table(rel_bias, patterns)
    blk = pl.BlockSpec((r, LANES), lambda p, b, ws, ty: (b, p))
    return pl.pallas_call(
        _na_kernel,
        grid_spec=pltpu.PrefetchScalarGridSpec(
            num_scalar_prefetch=2,
            grid=(C_HEADS // 2, BATCH),
            in_specs=[blk, blk, blk,
                      pl.BlockSpec((2, len(patterns), NA_BAND * GRID_W, win_rows * GRID_W),
                                   lambda p, b, ws, ty: (p, 0, 0, 0))],
            out_specs=blk,
        ),
        out_shape=jax.ShapeDtypeStruct((t_rows, C_WIDTH), BF16),
        compiler_params=_params("parallel", "parallel"),
        name="neighbourhood_attention",
    )(jnp.asarray(starts), jnp.asarray(types), q, k, v, bias)


def _out_kernel(n_lat, *refs):
    lat_refs = refs[:n_lat]
    w_ref, x_ref, g1_ref, g_ref, sh_ref, sc_ref, wr_ref, br_ref, xo_ref, h_ref, lg_ref = refs[n_lat:]
    y = None
    off = 0
    for lat in lat_refs:
        width = lat.shape[-1]
        part = _dot(lat[...], w_ref[off:off + width, :])
        y = part if y is None else y + part
        off += width
    x_new = x_ref[...] + g1_ref[0] * y
    xo_ref[...] = x_new
    h = _norm_modulate(x_new, g_ref[...], sh_ref[0], sc_ref[0])
    _store_token_tiles(h_ref, h)
    h_hi, h_lo = _split_bf16(h)
    both = _dot(h_hi, wr_ref[...])
    lg_ref[...] = (both[:, :ROUTE_LANES] + both[:, ROUTE_LANES:]
                   + _dot(h_lo, wr_ref[:, :ROUTE_LANES]) + br_ref[...])


def _router_weights(wg, bg, we, be):
    w = jnp.concatenate([wg, we], axis=1).astype(F32)
    w = jnp.pad(w, ((0, 0), (0, ROUTE_LANES - w.shape[1])))
    w_hi, w_lo = _split_bf16(w)
    b = jnp.pad(jnp.concatenate([bg, be]).astype(F32), (0, ROUTE_LANES - N_GROUPS - N_EXPERTS))
    return jnp.concatenate([w_hi, w_lo], axis=1), b.reshape(1, ROUTE_LANES)


def _out_proj(lats, w_out, xs, mod_l, norm_g, wr, br):
    t_rows = xs.shape[0]
    bpb = (CTX_LEN + SEQ) // ROW_TILE
    row = lambda w: pl.BlockSpec((ROW_TILE, w), lambda i: (i, 0))
    full = lambda a, b: pl.BlockSpec((a, b), lambda i: (0, 0))
    return pl.pallas_call(
        functools.partial(_out_kernel, len(lats)),
        grid=(t_rows // ROW_TILE,),
        in_specs=[row(l.shape[1]) for l in lats] + [
            full(*w_out.shape), row(D_MODEL), _mod_spec(2, bpb, BATCH), full(1, D_MODEL),
            _mod_spec(3, bpb, BATCH), _mod_spec(4, bpb, BATCH), full(D_MODEL, 2 * ROUTE_LANES), full(1, ROUTE_LANES),
        ],
        out_specs=[row(D_MODEL), pl.BlockSpec((ROW_TILE * TOKEN_SUBLANES, LANES), lambda i: (i, 0)),
                   row(ROUTE_LANES)],
        out_shape=[
            jax.ShapeDtypeStruct((t_rows, D_MODEL), F32),
            jax.ShapeDtypeStruct((t_rows * TOKEN_SUBLANES, LANES), F32),
            jax.ShapeDtypeStruct((t_rows, ROUTE_LANES), F32),
        ],
        compiler_params=_params("parallel"),
        name="out_proj_ffn_norm",
    )(*lats, w_out.astype(BF16), xs, mod_l, norm_g.reshape(1, D_MODEL), mod_l, mod_l, wr, br)


def _route(logits):
    t_rows = logits.shape[0]
    g_logits = logits[:, :N_GROUPS]
    e_logits = logits[:, N_GROUPS:N_GROUPS + N_EXPERTS].reshape(t_rows, N_GROUPS, E_PER_GROUP)
    g_idx = jnp.argmax(g_logits, axis=-1)
    g_w = jnp.take_along_axis(jax.nn.softmax(g_logits, axis=-1), g_idx[:, None], axis=1)[:, 0]
    e_sel = jnp.take_along_axis(e_logits, g_idx[:, None, None], axis=1)[:, 0]
    top_p, top_e = lax.top_k(jax.nn.softmax(e_sel, axis=-1), TOP_K)
    top_p = top_p / jnp.sum(top_p, axis=-1, keepdims=True)
    weights = g_w[:, None] * top_p
    flat_e = (g_idx[:, None] * E_PER_GROUP + top_e).astype(jnp.int32).reshape(-1)
    onehot = (flat_e[:, None] == jnp.arange(N_EXPERTS, dtype=jnp.int32)[None, :]).astype(jnp.int32)
    csum = jnp.cumsum(onehot, axis=0)
    rank = jnp.take_along_axis(csum, flat_e[:, None], axis=1)[:, 0] - 1
    counts = csum[-1]
    n_blk = (counts + MOE_ROWS - 1) // MOE_ROWS
    b_end = jnp.cumsum(n_blk)
    b_start = b_end - n_blk
    dest = (b_start[flat_e] * MOE_ROWS + rank).astype(jnp.int32).reshape(t_rows, TOP_K)
    n_used = b_end[-1].astype(jnp.int32)
    g_blocks = _moe_blocks(t_rows)
    blk = jnp.minimum(jnp.arange(g_blocks, dtype=jnp.int32), n_used - 1)
    block_e = jnp.minimum(jnp.searchsorted(b_end, blk, side='right'), N_EXPERTS - 1).astype(jnp.int32)
    return dest, weights.astype(F32), block_e, n_used.reshape(1)


def _moe_blocks(t_rows):
    return (t_rows * TOP_K) // MOE_ROWS + N_EXPERTS


def _slot_major(dest):
    t_rows = dest.shape[0]
    return dest.reshape(t_rows // ROW_TILE, ROW_TILE, TOP_K).transpose(0, 2, 1).reshape(
        t_rows // ROW_TILE, 1, TOP_K * ROW_TILE)


def _tile_copy(src_ref, src_token, dst_ref, dst_token, sem):
    src = src_ref.at[pl.ds(pl.multiple_of(src_token * TOKEN_SUBLANES, TOKEN_SUBLANES), TOKEN_SUBLANES)]
    dst = dst_ref.at[pl.ds(pl.multiple_of(dst_token * TOKEN_SUBLANES, TOKEN_SUBLANES), TOKEN_SUBLANES)]
    return pltpu.make_async_copy(src, dst, sem)


def _start_then_wait(copies):
    def start(r, c):
        for cp in copies(r):
            cp.start()
        return c

    def wait(r, c):
        for cp in copies(r):
            cp.wait()
        return c

    lax.fori_loop(0, ROW_TILE, start, 0, unroll=DMA_UNROLL)
    lax.fori_loop(0, ROW_TILE, wait, 0, unroll=DMA_UNROLL)


def _dispatch_kernel(dest_ref, h_ref, init_ref, xpad_ref, sem):
    del init_ref
    base = pl.program_id(0) * ROW_TILE
    _start_then_wait(lambda r: [
        _tile_copy(h_ref, base + r, xpad_ref, dest_ref[0, 0, k * ROW_TILE + r], sem) for k in range(TOP_K)])


def _dispatch(h_tiles, dest_blocks):
    t_rows = h_tiles.shape[0] // TOKEN_SUBLANES
    n_pad = _moe_blocks(t_rows) * MOE_ROWS
    init = jnp.zeros((n_pad * TOKEN_SUBLANES, LANES), F32)
    return pl.pallas_call(
        _dispatch_kernel,
        grid=(t_rows // ROW_TILE,),
        in_specs=[
            pl.BlockSpec((1, 1, TOP_K * ROW_TILE), lambda i: (i, 0, 0), memory_space=pltpu.SMEM),
            pl.BlockSpec(memory_space=pl.ANY),
            pl.BlockSpec(memory_space=pl.ANY),
        ],
        out_specs=pl.BlockSpec(memory_space=pl.ANY),
        out_shape=jax.ShapeDtypeStruct((n_pad * TOKEN_SUBLANES, LANES), F32),
        scratch_shapes=[pltpu.SemaphoreType.DMA(())],
        input_output_aliases={2: 0},
        compiler_params=_params("arbitrary"),
        name="moe_dispatch",
    )(dest_blocks, h_tiles, init)


def _expert_kernel(be_ref, nu_ref, x_ref, w1_ref, w3_ref, w2_ref, y_ref, w1_bf, w3_bf, w2_bf):
    i = pl.program_id(0)
    fresh = jnp.logical_or(i == 0, be_ref[i] != be_ref[jnp.maximum(i - 1, 0)])

    @pl.when(fresh)
    def _():
        w1_bf[...] = w1_ref[0, 0].astype(BF16)
        w3_bf[...] = w3_ref[0, 0].astype(BF16)
        w2_bf[...] = w2_ref[0, 0].astype(BF16)

    used = i < nu_ref[0]

    @pl.when(used)
    def _():
        xb = jnp.concatenate(
            [_load_token_tiles(x_ref, MOE_ROWS, s).astype(BF16) for s in range(TOKEN_SUBLANES)], axis=-1)
        hid = _silu(_dot(xb, w1_bf[...])) * _dot(xb, w3_bf[...])
        _store_token_tiles(y_ref, _dot(hid.astype(BF16), w2_bf[...]))

    @pl.when(jnp.logical_not(used))
    def _():
        y_ref[...] = jnp.zeros(y_ref.shape, F32)


def _experts(xpad, block_e, n_used, w1, w3, w2, layer):
    g_blocks = xpad.shape[0] // (MOE_ROWS * TOKEN_SUBLANES)
    tile_rows = MOE_ROWS * TOKEN_SUBLANES
    x_index = lambda i, be, nu: (jnp.minimum(i, nu[0] - 1), 0)
    w_index = lambda i, be, nu: (layer, be[i], 0, 0)
    return pl.pallas_call(
        _expert_kernel,
        grid_spec=pltpu.PrefetchScalarGridSpec(
            num_scalar_prefetch=2,
            grid=(g_blocks,),
            in_specs=[
                pl.BlockSpec((tile_rows, LANES), x_index),
                pl.BlockSpec((1, 1, D_MODEL, D_EXPERT), w_index),
                pl.BlockSpec((1, 1, D_MODEL, D_EXPERT), w_index),
                pl.BlockSpec((1, 1, D_EXPERT, D_MODEL), w_index),
            ],
            out_specs=pl.BlockSpec((tile_rows, LANES), lambda i, be, nu: (i, 0)),
            scratch_shapes=[
                pltpu.VMEM((D_MODEL, D_EXPERT), BF16),
                pltpu.VMEM((D_MODEL, D_EXPERT), BF16),
                pltpu.VMEM((D_EXPERT, D_MODEL), BF16),
            ],
        ),
        out_shape=jax.ShapeDtypeStruct((g_blocks * tile_rows, LANES), F32),
        compiler_params=_params("arbitrary"),
        name="moe_experts",
    )(block_e, n_used, xpad, w1, w3, w2)


def _combine_kernel(final, dest_ref, x_ref, wt_ref, g2_ref, fg_ref, ypad_ref, o_ref, ybuf, sem):
    _start_then_wait(lambda r: [
        _tile_copy(ypad_ref, dest_ref[0, 0, k * ROW_TILE + r], ybuf.at[k], r, sem) for k in range(TOP_K)])
    wt = wt_ref[...]
    parts = []
    for s in range(TOKEN_SUBLANES):
        lanes = slice(s * LANES, (s + 1) * LANES)
        f = (wt[:, 0:1] * _load_token_tiles(ybuf.at[0], ROW_TILE, s)
             + wt[:, 1:2] * _load_token_tiles(ybuf.at[1], ROW_TILE, s))
        parts.append(x_ref[:, lanes] + g2_ref[0][:, lanes] * f)
    x_new = jnp.concatenate(parts, axis=-1)
    if final:
        ms = jnp.mean(x_new * x_new, axis=-1, keepdims=True)
        x_new = x_new * lax.rsqrt(ms + EPS) * fg_ref[...]
    o_ref[...] = x_new


def _combine(ypad, dest_blocks, weights, xs, mod_l, final_g, final):
    t_rows = xs.shape[0]
    bpb = (CTX_LEN + SEQ) // ROW_TILE
    if final:
        xpb = SEQ // ROW_TILE
        n_out = BATCH * xpb
        blk = lambda i: (i // xpb) * bpb + i % xpb + 1
        g2_index = lambda i: (i // xpb, 0, 5)
    else:
        n_out = t_rows // ROW_TILE
        blk = lambda i: i
        g2_index = lambda i: (jnp.where(i % bpb == 0, BATCH, i // bpb), 0, 5)
    return pl.pallas_call(
        functools.partial(_combine_kernel, final),
        grid=(n_out,),
        in_specs=[
            pl.BlockSpec((1, 1, TOP_K * ROW_TILE), lambda i: (blk(i), 0, 0), memory_space=pltpu.SMEM),
            pl.BlockSpec((ROW_TILE, D_MODEL), lambda i: (blk(i), 0)),
            pl.BlockSpec((ROW_TILE, TOP_K), lambda i: (blk(i), 0)),
            pl.BlockSpec((1, 1, D_MODEL), g2_index),
            pl.BlockSpec((1, D_MODEL), lambda i: (0, 0)),
            pl.BlockSpec(memory_space=pl.ANY),
        ],
        out_specs=pl.BlockSpec((ROW_TILE, D_MODEL), lambda i: (i, 0)),
        out_shape=jax.ShapeDtypeStruct((n_out * ROW_TILE, D_MODEL), F32),
        scratch_shapes=[pltpu.VMEM((TOP_K, ROW_TILE * TOKEN_SUBLANES, LANES), F32), pltpu.SemaphoreType.DMA(())],
        compiler_params=_params("arbitrary"),
        name="moe_combine_final" if final else "moe_combine",
    )(dest_blocks, xs, weights, mod_l, final_g.reshape(1, D_MODEL), ypad)


def _moe(h, logits, xs, mod_l, w1, w3, w2, layer, final_g, final):
    dest, weights, block_e, n_used = _route(logits)
    dest_blocks = _slot_major(dest)
    xpad = _dispatch(h, dest_blocks)
    ypad = _experts(xpad, block_e, n_used, w1, w3, w2, layer)
    return _combine(ypad, dest_blocks, weights, xs, mod_l, final_g, final)


def kernel(x, c, ctx, c_ctx, ada_w, ada_b, norm_mix_g, norm_ffn_g, even_w_in, even_w_out, a_q_gain, a_k_gain, pool_w, pool_scale, odd_w_in, odd_w_out, na_rel_bias, moe_w_group, moe_b_group, moe_w_expert, moe_b_expert, moe_w1, moe_w3, moe_w2, final_g):
    r = CTX_LEN + SEQ
    xs = jnp.concatenate([ctx, x], axis=1).reshape(BATCH * r, D_MODEL)
    cvec = jnp.concatenate([c, c_ctx[None, :], jnp.zeros((MOD_ROWS - BATCH - 1, D_MODEL), F32)], axis=0)
    mod = _modulation(cvec, ada_w, ada_b)
    for l in range(DEPTH):
        mod_l = mod[l].reshape(MOD_ROWS, 1, N_MOD * D_MODEL)
        i = l // 2
        if l % 2 == 0:
            q, k, v, u = _in_even(xs, mod_l, norm_mix_g[l], even_w_in[i], a_q_gain[i], a_k_gain[i])
            lats = [_gqa(q, k, v), _pool(u, pool_w[i], pool_scale[i])]
            w_out = even_w_out[i]
        else:
            q, k, v = _in_odd(xs, mod_l, norm_mix_g[l], odd_w_in[i])
            lats = [_na(q, k, v, na_rel_bias[i])]
            w_out = odd_w_out[i]
        wr, br = _router_weights(moe_w_group[l], moe_b_group[l], moe_w_expert[l], moe_b_expert[l])
        xs, h, logits = _out_proj(lats, w_out, xs, mod_l, norm_ffn_g[l], wr, br)
        xs = _moe(h, logits, xs, mod_l, moe_w1, moe_w3, moe_w2, l, final_g, l == DEPTH - 1)
    return xs.reshape(BATCH, SEQ, D_MODEL)
```

```python
import functools

import jax
import jax.numpy as jnp
import numpy as np
from jax import lax
from jax.experimental import pallas as pl
from jax.experimental.pallas import tpu as pltpu

D_MODEL = 1024
BATCH = 8
SEQ = 4096
DEPTH = 2
GRID_W = 64
CTX_LEN = 256
EPS = 1e-6
N_MOD = 6

A_HEADS = 8
A_KV_HEADS = 2
A_HEAD_DIM = 64
A_WIDTH = A_HEADS * A_HEAD_DIM
A_KV_WIDTH = A_KV_HEADS * A_HEAD_DIM
A_SCALE = A_HEAD_DIM ** -0.5
ROPE_THETA = 10000.0
ROPE_FREQS = A_HEAD_DIM // 4

B_GROUPS = 4
B_WIDTH = D_MODEL // 2
B_GROUP_DIM = B_WIDTH // B_GROUPS
POOL_WINDOWS = (2, 4, 8, 16)
POOL_PAD = 16

C_HEADS = 16
C_HEAD_DIM = D_MODEL // C_HEADS
C_WIDTH = C_HEADS * C_HEAD_DIM
C_SCALE = C_HEAD_DIM ** -0.5
NA_ROWS_MAX = 8
NA_COLS = 16
NA_BAND = 4
LOG2_E = 1.4426950408889634

N_GROUPS = 4
E_PER_GROUP = 8
N_EXPERTS = N_GROUPS * E_PER_GROUP
TOP_K = 2
D_EXPERT = D_MODEL // 2

LANES = 128
TOKEN_SUBLANES = D_MODEL // LANES
ROW_TILE = 256
MOE_ROWS = 256
DMA_UNROLL = 8
MOD_ROWS = 16
MOD_TILE_N = 1536
ROUTE_LANES = 128
ROUTE_EXPERT_ROW = 8
VMEM_LIMIT_BYTES = 48 * 1024 * 1024
MASKED = -1e30

F32 = jnp.float32
BF16 = jnp.bfloat16


def _params(*semantics):
    return pltpu.CompilerParams(dimension_semantics=semantics, vmem_limit_bytes=VMEM_LIMIT_BYTES)


def _split_bf16(a):
    hi = a.astype(BF16)
    lo = (a - hi.astype(F32)).astype(BF16)
    return hi, lo


def _dot(a, b):
    return jnp.dot(a, b, preferred_element_type=F32)


def _dot_nt(a, b):
    return lax.dot_general(a, b, (((1,), (1,)), ((), ())), preferred_element_type=F32)


def _silu(a):
    return a * jax.nn.sigmoid(a)


def _mod_kernel(c_ref, w_ref, b_ref, o_ref):
    s_hi, s_lo = _split_bf16(_silu(c_ref[...]))
    w_hi, w_lo = _split_bf16(w_ref[0])
    o_ref[0] = _dot(s_hi, w_hi) + _dot(s_hi, w_lo) + _dot(s_lo, w_hi) + b_ref[0]


def _modulation(cvec, ada_w, ada_b):
    depth, d, n = ada_w.shape
    return pl.pallas_call(
        _mod_kernel,
        grid=(depth, n // MOD_TILE_N),
        in_specs=[
            pl.BlockSpec((MOD_ROWS, d), lambda l, j: (0, 0)),
            pl.BlockSpec((1, d, MOD_TILE_N), lambda l, j: (l, 0, j)),
            pl.BlockSpec((1, 1, MOD_TILE_N), lambda l, j: (l, 0, j)),
        ],
        out_specs=pl.BlockSpec((1, MOD_ROWS, MOD_TILE_N), lambda l, j: (l, 0, j)),
        out_shape=jax.ShapeDtypeStruct((depth, MOD_ROWS, n), F32),
        compiler_params=_params("parallel", "parallel"),
        name="adaln_mod",
    )(cvec, ada_w, ada_b.reshape(depth, 1, n))


def _mod_spec(chunk, blocks_per_batch, ctx_row):
    def index(i):
        return (jnp.where(i % blocks_per_batch == 0, ctx_row, i // blocks_per_batch), 0, chunk)
    return pl.BlockSpec((1, 1, D_MODEL), index)


def _store_token_tiles(ref, rows):
    n = rows.shape[0]
    for s in range(TOKEN_SUBLANES):
        ref[pl.ds(s, n, stride=TOKEN_SUBLANES), :] = rows[:, s * LANES:(s + 1) * LANES]


def _load_token_tiles(ref, n, s):
    return ref[pl.ds(s, n, stride=TOKEN_SUBLANES), :]


def _norm_modulate(x, g, shift, scale):
    ms = jnp.mean(x * x, axis=-1, keepdims=True)
    return (x * lax.rsqrt(ms + EPS) * g) * (1.0 + scale) + shift


def _qk_norm_rope(a, gmat_ref, perm_ref, cos_ref, sin_ref):
    w = a.shape[-1]
    sq_hi, sq_lo = _split_bf16(a * a)
    gmat = gmat_ref[:w, :w]
    msq = _dot(sq_hi, gmat) + _dot(sq_lo, gmat)
    rot = _dot(a.astype(BF16), perm_ref[:w, :w])
    return lax.rsqrt(msq + EPS) * (a * cos_ref[...] + rot * sin_ref[...])


def _in_even_kernel(x_ref, g_ref, sh_ref, sc_ref, w_ref, cq_ref, sq_ref, ck_ref, sk_ref, gmat_ref, perm_ref,
                    q_ref, k_ref, v_ref, u_ref):
    hb = _norm_modulate(x_ref[...], g_ref[...], sh_ref[0], sc_ref[0]).astype(BF16)
    c0, c1, c2 = A_WIDTH, A_WIDTH + A_KV_WIDTH, A_WIDTH + 2 * A_KV_WIDTH
    q = _dot(hb, w_ref[:, :c0])
    q_ref[...] = _qk_norm_rope(q, gmat_ref, perm_ref, cq_ref, sq_ref).astype(BF16)
    k = _dot(hb, w_ref[:, c0:c1])
    k_ref[...] = _qk_norm_rope(k, gmat_ref, perm_ref, ck_ref, sk_ref).astype(BF16)
    v_ref[...] = _dot(hb, w_ref[:, c1:c2]).astype(BF16)
    u_ref[...] = _dot(hb, w_ref[:, c2:])


def _rope_tables(q_gain, k_gain):
    t = np.arange(SEQ)
    pos = np.stack([t // GRID_W, t % GRID_W], axis=-1).astype(np.float32)
    inv = (ROPE_THETA ** (-np.arange(ROPE_FREQS, dtype=np.float32) / ROPE_FREQS)).astype(np.float32)
    ang = pos[:, :, None] * inv
    cos = np.concatenate([np.cos(ang), np.cos(ang)], axis=-1).reshape(SEQ, A_HEAD_DIM)
    sin = np.concatenate([-np.sin(ang), np.sin(ang)], axis=-1).reshape(SEQ, A_HEAD_DIM)
    cos = np.concatenate([np.ones((CTX_LEN, A_HEAD_DIM), np.float32), cos.astype(np.float32)], axis=0)
    sin = np.concatenate([np.zeros((CTX_LEN, A_HEAD_DIM), np.float32), sin.astype(np.float32)], axis=0)
    swap = _rope_swap()
    def tables(gain, heads, scale):
        c = jnp.asarray(cos) * (gain * scale)[None, :]
        s = jnp.asarray(sin) * (gain[swap] * scale)[None, :]
        return jnp.tile(c, (1, heads)), jnp.tile(s, (1, heads))
    cq, sq = tables(q_gain, A_HEADS, A_SCALE * LOG2_E)
    ck, sk = tables(k_gain, A_KV_HEADS, 1.0)
    return cq, sq, ck, sk


def _rope_swap():
    j = np.arange(A_HEAD_DIM)
    half = (j // ROPE_FREQS) % 2
    return np.where(half == 0, j + ROPE_FREQS, j - ROPE_FREQS)


def _head_matrices():
    lane = np.arange(A_WIDTH)
    same_head = (lane[:, None] // A_HEAD_DIM) == (lane[None, :] // A_HEAD_DIM)
    gmat = same_head.astype(np.float32) / A_HEAD_DIM
    swap = np.tile(_rope_swap(), A_HEADS) + (lane // A_HEAD_DIM) * A_HEAD_DIM
    perm = np.zeros((A_WIDTH, A_WIDTH), np.float32)
    perm[swap, lane] = 1.0
    return jnp.asarray(gmat, BF16), jnp.asarray(perm, BF16)


def _in_even(xs, mod_l, norm_g, w_in, q_gain, k_gain):
    t_rows = xs.shape[0]
    bpb = (CTX_LEN + SEQ) // ROW_TILE
    cq, sq, ck, sk = _rope_tables(q_gain, k_gain)
    gmat, perm = _head_matrices()
    n_in = w_in.shape[1]
    row = lambda w: pl.BlockSpec((ROW_TILE, w), lambda i: (i, 0))
    tab = lambda w: pl.BlockSpec((ROW_TILE, w), lambda i: (i % bpb, 0))
    full = lambda a, b: pl.BlockSpec((a, b), lambda i: (0, 0))
    return pl.pallas_call(
        _in_even_kernel,
        grid=(t_rows // ROW_TILE,),
        in_specs=[
            row(D_MODEL), full(1, D_MODEL), _mod_spec(0, bpb, BATCH), _mod_spec(1, bpb, BATCH),
            full(D_MODEL, n_in), tab(A_WIDTH), tab(A_WIDTH), tab(A_KV_WIDTH), tab(A_KV_WIDTH),
            full(A_WIDTH, A_WIDTH), full(A_WIDTH, A_WIDTH),
        ],
        out_specs=[row(A_WIDTH), row(A_KV_WIDTH), row(A_KV_WIDTH), row(B_WIDTH)],
        out_shape=[
            jax.ShapeDtypeStruct((t_rows, A_WIDTH), BF16),
            jax.ShapeDtypeStruct((t_rows, A_KV_WIDTH), BF16),
            jax.ShapeDtypeStruct((t_rows, A_KV_WIDTH), BF16),
            jax.ShapeDtypeStruct((t_rows, B_WIDTH), F32),
        ],
        compiler_params=_params("parallel"),
        name="even_in_proj",
    )(xs, norm_g.reshape(1, D_MODEL), mod_l, mod_l, w_in.astype(BF16), cq, sq, ck, sk, gmat, perm)


def _gqa_kernel(q_ref, k_ref, v_ref, o_ref):
    def attend(n_keys):
        k = k_ref[0, 0, :n_keys, :]
        v = v_ref[0, 0, :n_keys, :]
        outs = []
        for h in range(2):
            s = _dot_nt(q_ref[:, h * A_HEAD_DIM:(h + 1) * A_HEAD_DIM], k)
            m = jnp.max(s, axis=-1, keepdims=True)
            p = jnp.exp2(s - m).astype(BF16)
            acc = _dot(p, v)
            outs.append(acc[:, :A_HEAD_DIM] / acc[:, A_HEAD_DIM:A_HEAD_DIM + 1])
        o_ref[...] = jnp.concatenate(outs, axis=-1).astype(BF16)

    is_ctx = pl.program_id(1) == 0

    @pl.when(is_ctx)
    def _():
        attend(CTX_LEN)

    @pl.when(jnp.logical_not(is_ctx))
    def _():
        attend(CTX_LEN + SEQ)


def _gqa(q, k, v):
    t_rows = q.shape[0]
    r = CTX_LEN + SEQ
    bpb = r // ROW_TILE
    group = A_HEADS // A_KV_HEADS
    kh = k.reshape(BATCH, r, A_KV_HEADS, A_HEAD_DIM).transpose(0, 2, 1, 3)
    vh = v.reshape(BATCH, r, A_KV_HEADS, A_HEAD_DIM).transpose(0, 2, 1, 3)
    ones = jnp.ones((BATCH, A_KV_HEADS, r, 1), BF16)
    zeros = jnp.zeros((BATCH, A_KV_HEADS, r, LANES - A_HEAD_DIM - 1), BF16)
    vext = jnp.concatenate([vh, ones, zeros], axis=-1)
    pairs = A_HEADS // 2
    return pl.pallas_call(
        _gqa_kernel,
        grid=(BATCH, bpb, pairs),
        in_specs=[
            pl.BlockSpec((ROW_TILE, LANES), lambda b, j, p: (b * bpb + j, p)),
            pl.BlockSpec((1, 1, r, A_HEAD_DIM), lambda b, j, p: (b, (2 * p) // group, 0, 0)),
            pl.BlockSpec((1, 1, r, LANES), lambda b, j, p: (b, (2 * p) // group, 0, 0)),
        ],
        out_specs=pl.BlockSpec((ROW_TILE, LANES), lambda b, j, p: (b * bpb + j, p)),
        out_shape=jax.ShapeDtypeStruct((t_rows, A_WIDTH), BF16),
        compiler_params=_params("parallel", "parallel", "parallel"),
        name="gqa_attention",
    )(q, kh, vext)


def _pool_kernel(u_ref, w_ref, s_ref, o_ref, pad_ref):
    g = pl.program_id(1)
    w_bf = w_ref[0].astype(BF16)
    for start, length in ((0, CTX_LEN), (CTX_LEN, SEQ)):
        u = u_ref[start:start + length, :]
        pad_ref[0:POOL_PAD, :] = jnp.zeros((POOL_PAD, LANES), F32)
        pad_ref[POOL_PAD:POOL_PAD + length, :] = u
        pad_ref[POOL_PAD + length:2 * POOL_PAD + length, :] = jnp.zeros((POOL_PAD, LANES), F32)
        t = lax.broadcasted_iota(jnp.int32, (length, LANES), 0)
        for gi, win in enumerate(POOL_WINDOWS):
            @pl.when(g == gi)
            def _(win=win, u=u, t=t, start=start, length=length):
                base = POOL_PAD - win // 2
                acc = pad_ref[base:base + length, :]
                for j in range(1, win):
                    acc = acc + pad_ref[base + j:base + j + length, :]
                lo = jnp.clip(t - win // 2, 0, length)
                hi = jnp.clip(t - win // 2 + win, 0, length)
                pooled = acc / (hi - lo).astype(F32) - u
                mixed = _dot(pooled.astype(BF16), w_bf) * s_ref[0]
                o_ref[start:start + length, :] = mixed.astype(BF16)


def _pool(u, pool_w, pool_scale):
    t_rows = u.shape[0]
    r = CTX_LEN + SEQ
    return pl.pallas_call(
        _pool_kernel,
        grid=(BATCH, B_GROUPS),
        in_specs=[
            pl.BlockSpec((r, B_GROUP_DIM), lambda b, g: (b, g)),
            pl.BlockSpec((1, B_GROUP_DIM, B_GROUP_DIM), lambda b, g: (g, 0, 0)),
            pl.BlockSpec((1, 1, B_GROUP_DIM), lambda b, g: (g, 0, 0)),
        ],
        out_specs=pl.BlockSpec((r, B_GROUP_DIM), lambda b, g: (b, g)),
        out_shape=jax.ShapeDtypeStruct((t_rows, B_WIDTH), BF16),
        scratch_shapes=[pltpu.VMEM((SEQ + 2 * POOL_PAD, B_GROUP_DIM), F32)],
        compiler_params=_params("parallel", "parallel"),
        name="pool_mixer",
    )(u, pool_w, pool_scale.reshape(B_GROUPS, 1, B_GROUP_DIM))


def _in_odd_kernel(x_ref, g_ref, sh_ref, sc_ref, w_ref, q_ref, k_ref, v_ref):
    hb = _norm_modulate(x_ref[...], g_ref[...], sh_ref[0], sc_ref[0]).astype(BF16)
    q_ref[...] = (_dot(hb, w_ref[:, :C_WIDTH]) * (C_SCALE * LOG2_E)).astype(BF16)
    k_ref[...] = _dot(hb, w_ref[:, C_WIDTH:2 * C_WIDTH]).astype(BF16)
    v_ref[...] = _dot(hb, w_ref[:, 2 * C_WIDTH:]).astype(BF16)


def _in_odd(xs, mod_l, norm_g, w_in):
    t_rows = xs.shape[0]
    bpb = (CTX_LEN + SEQ) // ROW_TILE
    row = lambda w: pl.BlockSpec((ROW_TILE, w), lambda i: (i, 0))
    full = lambda a, b: pl.BlockSpec((a, b), lambda i: (0, 0))
    return pl.pallas_call(
        _in_odd_kernel,
        grid=(t_rows // ROW_TILE,),
        in_specs=[row(D_MODEL), full(1, D_MODEL), _mod_spec(0, bpb, BATCH), _mod_spec(1, bpb, BATCH),
                  full(D_MODEL, 3 * C_WIDTH)],
        out_specs=[row(C_WIDTH)] * 3,
        out_shape=[jax.ShapeDtypeStruct((t_rows, C_WIDTH), BF16)] * 3,
        compiler_params=_params("parallel"),
        name="odd_in_proj",
    )(xs, norm_g.reshape(1, D_MODEL), mod_l, mod_l, w_in.astype(BF16))


def _na_plan():
    rows = SEQ // GRID_W
    kh = min(NA_ROWS_MAX, rows)
    win_rows = min(kh + NA_BAND - 1, rows)
    starts, types, patterns = [], [], []
    for r0 in range(0, rows, NA_BAND):
        ws = int(np.clip(r0 - kh // 2, 0, rows - win_rows))
        pat = np.full((NA_BAND, win_rows), -1, np.int64)
        for rr in range(NA_BAND):
            r = r0 + rr
            rs = int(np.clip(r - kh // 2, 0, rows - kh))
            for wi in range(win_rows):
                if rs <= ws + wi < rs + kh:
                    pat[rr, wi] = ws + wi - r + (NA_ROWS_MAX - 1)
        for t, known in enumerate(patterns):
            if np.array_equal(known, pat):
                types.append(t)
                break
        else:
            types.append(len(patterns))
            patterns.append(pat)
        starts.append(ws)
    return win_rows, np.asarray(starts, np.int32), np.asarray(types, np.int32), patterns


def _na_bias_table(rel_bias, patterns):
    n_off = 2 * NA_ROWS_MAX - 1
    cq = np.arange(GRID_W)
    col_start = np.clip(cq - NA_COLS // 2, 0, GRID_W - NA_COLS)
    ck = np.arange(GRID_W)
    inside = (ck[None, :] >= col_start[:, None]) & (ck[None, :] < col_start[:, None] + NA_COLS)
    shift = GRID_W - NA_COLS
    padded = jnp.pad(rel_bias.astype(F32) * LOG2_E, ((0, 0), (0, 0), (shift, shift)))
    tcol = jnp.stack([padded[:, :, GRID_W - 1 - c:2 * GRID_W - 1 - c] for c in range(GRID_W)], axis=2)
    tcol = jnp.where(jnp.asarray(inside)[None, None], tcol, MASKED)
    tcol = jnp.concatenate([tcol, jnp.full((C_HEADS, 1, GRID_W, GRID_W), MASKED, F32)], axis=1)
    tables = []
    for pat in patterns:
        idx = np.where(pat < 0, n_off, pat)
        blocks = tcol[:, idx].transpose(0, 1, 3, 2, 4)
        tables.append(blocks.reshape(C_HEADS, idx.shape[0] * GRID_W, idx.shape[1] * GRID_W))
    return jnp.stack(tables, axis=1)


def _na_kernel(ws_ref, ty_ref, q_ref, k_ref, v_ref, bias_ref, o_ref):
    n_bands = (SEQ // GRID_W) // NA_BAND
    band_q = NA_BAND * GRID_W
    win = bias_ref.shape[-1]
    o_ref[0:CTX_LEN, :] = jnp.zeros((CTX_LEN, LANES), BF16)
    kc = k_ref[0:CTX_LEN, :]
    vc = v_ref[0:CTX_LEN, :]
    first = lax.broadcasted_iota(jnp.int32, (1, LANES), 1) < C_HEAD_DIM

    def one_band(b, carry):
        q_off = pl.multiple_of(CTX_LEN + b * band_q, GRID_W)
        k_off = pl.multiple_of(CTX_LEN + ws_ref[b] * GRID_W, GRID_W)
        qb = q_ref[pl.ds(q_off, band_q), :]
        kw = k_ref[pl.ds(k_off, win), :]
        vw = v_ref[pl.ds(k_off, win), :]
        outs = []
        for h in range(2):
            qm = jnp.where(first if h == 0 else jnp.logical_not(first), qb, jnp.zeros_like(qb))
            s_loc = _dot_nt(qm, kw) + bias_ref[h, ty_ref[b]]
            s_ctx = _dot_nt(qm, kc)
            m = jnp.maximum(jnp.max(s_loc, axis=-1, keepdims=True), jnp.max(s_ctx, axis=-1, keepdims=True))
            p_loc = jnp.exp2(s_loc - m)
            p_ctx = jnp.exp2(s_ctx - m)
            denom = jnp.sum(p_loc, axis=-1, keepdims=True) + jnp.sum(p_ctx, axis=-1, keepdims=True)
            o = _dot(p_loc.astype(BF16), vw) + _dot(p_ctx.astype(BF16), vc)
            outs.append(o / denom)
        o_ref[pl.ds(q_off, band_q), :] = jnp.where(first, outs[0], outs[1]).astype(BF16)
        return carry

    lax.fori_loop(0, n_bands, one_band, 0, unroll=2)


def _na(q, k, v, rel_bias):
    t_rows = q.shape[0]
    r = CTX_LEN + SEQ
    win_rows, starts, types, patterns = _na_plan()
    bias = _na_bias_table(rel_bias, patterns)
    blk = pl.BlockSpec((r, LANES), lambda p, b, ws, ty: (b, p))
    return pl.pallas_call(
        _na_kernel,
        grid_spec=pltpu.PrefetchScalarGridSpec(
            num_scalar_prefetch=2,
            grid=(C_HEADS // 2, BATCH),
            in_specs=[blk, blk, blk,
                      pl.BlockSpec((2, len(patterns), NA_BAND * GRID_W, win_rows * GRID_W),
                                   lambda p, b, ws, ty: (p, 0, 0, 0))],
            out_specs=blk,
        ),
        out_shape=jax.ShapeDtypeStruct((t_rows, C_WIDTH), BF16),
        compiler_params=_params("parallel", "parallel"),
        name="neighbourhood_attention",
    )(jnp.asarray(starts), jnp.asarray(types), q, k, v, bias)


def _out_kernel(n_lat, *refs):
    lat_refs = refs[:n_lat]
    w_ref, x_ref, g1_ref, g_ref, sh_ref, sc_ref, wr_ref, br_ref, xo_ref, h_ref, lg_ref = refs[n_lat:]
    y = None
    off = 0
    for lat in lat_refs:
        width = lat.shape[-1]
        part = _dot(lat[...], w_ref[off:off + width, :])
        y = part if y is None else y + part
        off += width
    x_new = x_ref[...] + g1_ref[0] * y
    xo_ref[...] = x_new
    h = _norm_modulate(x_new, g_ref[...], sh_ref[0], sc_ref[0])
    _store_token_tiles(h_ref, h)
    h_hi, h_lo = _split_bf16(h)
    both = _dot(h_hi, wr_ref[...])
    lg_ref[...] = (both[:, :ROUTE_LANES] + both[:, ROUTE_LANES:]
                   + _dot(h_lo, wr_ref[:, :ROUTE_LANES]) + br_ref[...])


def _router_weights(wg, bg, we, be):
    gap = ROUTE_EXPERT_ROW - N_GROUPS
    tail = ROUTE_LANES - ROUTE_EXPERT_ROW - N_EXPERTS
    d = wg.shape[0]
    w = jnp.concatenate([wg, jnp.zeros((d, gap), F32), we, jnp.zeros((d, tail), F32)], axis=1)
    w_hi, w_lo = _split_bf16(w)
    b = jnp.concatenate([bg, jnp.zeros((gap,), F32), be, jnp.zeros((tail,), F32)])
    return jnp.concatenate([w_hi, w_lo], axis=1), b.reshape(1, ROUTE_LANES)


def _out_proj(lats, w_out, xs, mod_l, norm_g, wr, br):
    t_rows = xs.shape[0]
    bpb = (CTX_LEN + SEQ) // ROW_TILE
    row = lambda w: pl.BlockSpec((ROW_TILE, w), lambda i: (i, 0))
    full = lambda a, b: pl.BlockSpec((a, b), lambda i: (0, 0))
    return pl.pallas_call(
        functools.partial(_out_kernel, len(lats)),
        grid=(t_rows // ROW_TILE,),
        in_specs=[row(l.shape[1]) for l in lats] + [
            full(*w_out.shape), row(D_MODEL), _mod_spec(2, bpb, BATCH), full(1, D_MODEL),
            _mod_spec(3, bpb, BATCH), _mod_spec(4, bpb, BATCH), full(D_MODEL, 2 * ROUTE_LANES), full(1, ROUTE_LANES),
        ],
        out_specs=[row(D_MODEL), pl.BlockSpec((ROW_TILE * TOKEN_SUBLANES, LANES), lambda i: (i, 0)),
                   row(ROUTE_LANES)],
        out_shape=[
            jax.ShapeDtypeStruct((t_rows, D_MODEL), F32),
            jax.ShapeDtypeStruct((t_rows * TOKEN_SUBLANES, LANES), F32),
            jax.ShapeDtypeStruct((t_rows, ROUTE_LANES), F32),
        ],
        compiler_params=_params("parallel"),
        name="out_proj_ffn_norm",
    )(*lats, w_out.astype(BF16), xs, mod_l, norm_g.reshape(1, D_MODEL), mod_l, mod_l, wr, br)


def _route_kernel(lg_ref, tri_ref, e_ref, rk_ref, wt_ref, cnt_ref, run_ref):
    @pl.when(pl.program_id(0) == 0)
    def _():
        run_ref[...] = jnp.zeros(run_ref.shape, F32)

    lt = lg_ref[...].T
    sub = lax.broadcasted_iota(jnp.int32, (E_PER_GROUP, ROW_TILE), 0)
    gl = jnp.where(sub < N_GROUPS, lt[0:E_PER_GROUP], -jnp.inf)
    g_max = jnp.max(gl, axis=0, keepdims=True)
    g_idx = jnp.min(jnp.where(gl == g_max, sub, E_PER_GROUP), axis=0, keepdims=True)
    g_w = 1.0 / jnp.sum(jnp.exp(gl - g_max), axis=0, keepdims=True)
    sel = lt[ROUTE_EXPERT_ROW:ROUTE_EXPERT_ROW + E_PER_GROUP]
    for g in range(1, N_GROUPS):
        lo = ROUTE_EXPERT_ROW + g * E_PER_GROUP
        sel = jnp.where(g_idx == g, lt[lo:lo + E_PER_GROUP], sel)
    m1 = jnp.max(sel, axis=0, keepdims=True)
    i1 = jnp.min(jnp.where(sel == m1, sub, E_PER_GROUP), axis=0, keepdims=True)
    rest = jnp.where(sub == i1, -jnp.inf, sel)
    m2 = jnp.max(rest, axis=0, keepdims=True)
    i2 = jnp.min(jnp.where(rest == m2, sub, E_PER_GROUP), axis=0, keepdims=True)
    t = jnp.exp(m2 - m1)
    p1 = 1.0 / (1.0 + t)
    wt_ref[0] = jnp.concatenate([g_w * p1, g_w * (t * p1)], axis=0)
    e_flat = jnp.concatenate([g_idx * E_PER_GROUP + i1, g_idx * E_PER_GROUP + i2], axis=1)
    e_ref[0] = e_flat
    hit = lax.broadcasted_iota(jnp.int32, (N_EXPERTS, TOP_K * ROW_TILE), 0) == e_flat
    onehot = jnp.where(hit, 1.0, 0.0)
    before = _dot(onehot.astype(BF16), tri_ref[...]) + run_ref[:, 0:1]
    rk_ref[0] = jnp.sum(jnp.where(hit, before, 0.0), axis=0, keepdims=True).astype(jnp.int32)
    run_ref[...] = run_ref[...] + jnp.sum(onehot, axis=1, keepdims=True)
    cnt_ref[...] = run_ref[...]


def _route(logits):
    t_rows = logits.shape[0]
    n_blk = t_rows // ROW_TILE
    n_asg = TOP_K * ROW_TILE
    tri = jnp.asarray(np.triu(np.ones((n_asg, n_asg), np.float32), 1), BF16)
    e_blk, rank_blk, wt_blk, counts = pl.pallas_call(
        _route_kernel,
        grid=(n_blk,),
        in_specs=[pl.BlockSpec((ROW_TILE, ROUTE_LANES), lambda i: (i, 0)),
                  pl.BlockSpec((n_asg, n_asg), lambda i: (0, 0))],
        out_specs=[pl.BlockSpec((1, 1, n_asg), lambda i: (i, 0, 0)),
                   pl.BlockSpec((1, 1, n_asg), lambda i: (i, 0, 0)),
                   pl.BlockSpec((1, TOP_K, ROW_TILE), lambda i: (i, 0, 0)),
                   pl.BlockSpec((N_EXPERTS, LANES), lambda i: (0, 0))],
        out_shape=[jax.ShapeDtypeStruct((n_blk, 1, n_asg), jnp.int32),
                   jax.ShapeDtypeStruct((n_blk, 1, n_asg), jnp.int32),
                   jax.ShapeDtypeStruct((n_blk, TOP_K, ROW_TILE), F32),
                   jax.ShapeDtypeStruct((N_EXPERTS, LANES), F32)],
        scratch_shapes=[pltpu.VMEM((N_EXPERTS, LANES), F32)],
        compiler_params=_params("arbitrary"),
        name="moe_route",
    )(logits, tri)
    counts = counts[:, 0].astype(jnp.int32)
    blocks_of = (counts + MOE_ROWS - 1) // MOE_ROWS
    b_end = jnp.cumsum(blocks_of)
    slot_base = ((b_end - blocks_of) * MOE_ROWS).astype(jnp.int32)
    n_used = b_end[-1].astype(jnp.int32)
    blk = jnp.minimum(jnp.arange(_moe_blocks(t_rows), dtype=jnp.int32), n_used - 1)
    block_e = jnp.minimum(jnp.searchsorted(b_end, blk, side='right'), N_EXPERTS - 1).astype(jnp.int32)
    weights = wt_blk.transpose(0, 2, 1).reshape(t_rows, TOP_K)
    return (slot_base, e_blk, rank_blk), weights, block_e, n_used.reshape(1)


def _moe_blocks(t_rows):
    return (t_rows * TOP_K) // MOE_ROWS + N_EXPERTS


def _tile_copy(src_ref, src_token, dst_ref, dst_token, sem):
    src = src_ref.at[pl.ds(pl.multiple_of(src_token * TOKEN_SUBLANES, TOKEN_SUBLANES), TOKEN_SUBLANES)]
    dst = dst_ref.at[pl.ds(pl.multiple_of(dst_token * TOKEN_SUBLANES, TOKEN_SUBLANES), TOKEN_SUBLANES)]
    return pltpu.make_async_copy(src, dst, sem)


def _start_then_wait(copies):
    def start(r, c):
        for cp in copies(r):
            cp.start()
        return c

    def wait(r, c):
        for cp in copies(r):
            cp.wait()
        return c

    lax.fori_loop(0, ROW_TILE, start, 0, unroll=DMA_UNROLL)
    lax.fori_loop(0, ROW_TILE, wait, 0, unroll=DMA_UNROLL)


def _slot(base_ref, e_ref, rk_ref, j):
    return base_ref[e_ref[0, 0, j]] + rk_ref[0, 0, j]


def _dispatch_kernel(base_ref, e_ref, rk_ref, h_ref, init_ref, xpad_ref, sem):
    del init_ref
    _start_then_wait(lambda r: [
        _tile_copy(h_ref, r, xpad_ref, _slot(base_ref, e_ref, rk_ref, k * ROW_TILE + r), sem)
        for k in range(TOP_K)])


def _dispatch(h_tiles, slots):
    slot_base, e_blk, rank_blk = slots
    t_rows = h_tiles.shape[0] // TOKEN_SUBLANES
    n_pad = _moe_blocks(t_rows) * MOE_ROWS
    init = jnp.zeros((n_pad * TOKEN_SUBLANES, LANES), F32)
    asg = pl.BlockSpec((1, 1, TOP_K * ROW_TILE), lambda i, base: (i, 0, 0), memory_space=pltpu.SMEM)
    return pl.pallas_call(
        _dispatch_kernel,
        grid_spec=pltpu.PrefetchScalarGridSpec(
            num_scalar_prefetch=1,
            grid=(t_rows // ROW_TILE,),
            in_specs=[asg, asg,
                      pl.BlockSpec((ROW_TILE * TOKEN_SUBLANES, LANES), lambda i, base: (i, 0)),
                      pl.BlockSpec(memory_space=pl.ANY)],
            out_specs=pl.BlockSpec(memory_space=pl.ANY),
            scratch_shapes=[pltpu.SemaphoreType.DMA(())],
        ),
        out_shape=jax.ShapeDtypeStruct((n_pad * TOKEN_SUBLANES, LANES), F32),
        input_output_aliases={4: 0},
        compiler_params=_params("arbitrary"),
        name="moe_dispatch",
    )(slot_base, e_blk, rank_blk, h_tiles, init)


def _expert_kernel(be_ref, nu_ref, x_ref, w1_ref, w3_ref, w2_ref, y_ref, w1_bf, w3_bf, w2_bf):
    i = pl.program_id(0)
    fresh = jnp.logical_or(i == 0, be_ref[i] != be_ref[jnp.maximum(i - 1, 0)])

    @pl.when(fresh)
    def _():
        w1_bf[...] = w1_ref[0, 0].astype(BF16)
        w3_bf[...] = w3_ref[0, 0].astype(BF16)
        w2_bf[...] = w2_ref[0, 0].astype(BF16)

    used = i < nu_ref[0]

    @pl.when(used)
    def _():
        xb = jnp.concatenate(
            [_load_token_tiles(x_ref, MOE_ROWS, s).astype(BF16) for s in range(TOKEN_SUBLANES)], axis=-1)
        hid = _silu(_dot(xb, w1_bf[...])) * _dot(xb, w3_bf[...])
        _store_token_tiles(y_ref, _dot(hid.astype(BF16), w2_bf[...]))

    @pl.when(jnp.logical_not(used))
    def _():
        y_ref[...] = jnp.zeros(y_ref.shape, F32)


def _experts(xpad, block_e, n_used, w1, w3, w2, layer):
    g_blocks = xpad.shape[0] // (MOE_ROWS * TOKEN_SUBLANES)
    tile_rows = MOE_ROWS * TOKEN_SUBLANES
    x_index = lambda i, be, nu: (jnp.minimum(i, nu[0] - 1), 0)
    w_index = lambda i, be, nu: (layer, be[i], 0, 0)
    return pl.pallas_call(
        _expert_kernel,
        grid_spec=pltpu.PrefetchScalarGridSpec(
            num_scalar_prefetch=2,
            grid=(g_blocks,),
            in_specs=[
                pl.BlockSpec((tile_rows, LANES), x_index),
                pl.BlockSpec((1, 1, D_MODEL, D_EXPERT), w_index),
                pl.BlockSpec((1, 1, D_MODEL, D_EXPERT), w_index),
                pl.BlockSpec((1, 1, D_EXPERT, D_MODEL), w_index),
            ],
            out_specs=pl.BlockSpec((tile_rows, LANES), lambda i, be, nu: (i, 0)),
            scratch_shapes=[
                pltpu.VMEM((D_MODEL, D_EXPERT), BF16),
                pltpu.VMEM((D_MODEL, D_EXPERT), BF16),
                pltpu.VMEM((D_EXPERT, D_MODEL), BF16),
            ],
        ),
        out_shape=jax.ShapeDtypeStruct((g_blocks * tile_rows, LANES), F32),
        compiler_params=_params("arbitrary"),
        name="moe_experts",
    )(block_e, n_used, xpad, w1, w3, w2)


def _combine_kernel(final, base_ref, e_ref, rk_ref, x_ref, wt_ref, g2_ref, fg_ref, ypad_ref, o_ref, ybuf, sem):
    _start_then_wait(lambda r: [
        _tile_copy(ypad_ref, _slot(base_ref, e_ref, rk_ref, k * ROW_TILE + r), ybuf.at[k], r, sem)
        for k in range(TOP_K)])
    wt = wt_ref[...]
    parts = []
    for s in range(TOKEN_SUBLANES):
        lanes = slice(s * LANES, (s + 1) * LANES)
        f = (wt[:, 0:1] * _load_token_tiles(ybuf.at[0], ROW_TILE, s)
             + wt[:, 1:2] * _load_token_tiles(ybuf.at[1], ROW_TILE, s))
        parts.append(x_ref[:, lanes] + g2_ref[0][:, lanes] * f)
    x_new = jnp.concatenate(parts, axis=-1)
    if final:
        ms = jnp.mean(x_new * x_new, axis=-1, keepdims=True)
        x_new = x_new * lax.rsqrt(ms + EPS) * fg_ref[...]
    o_ref[...] = x_new


def _combine(ypad, slots, weights, xs, mod_l, final_g, final):
    slot_base, e_blk, rank_blk = slots
    t_rows = xs.shape[0]
    bpb = (CTX_LEN + SEQ) // ROW_TILE
    if final:
        xpb = SEQ // ROW_TILE
        n_out = BATCH * xpb
        blk = lambda i: (i // xpb) * bpb + i % xpb + 1
        g2_index = lambda i, base: (i // xpb, 0, 5)
    else:
        n_out = t_rows // ROW_TILE
        blk = lambda i: i
        g2_index = lambda i, base: (jnp.where(i % bpb == 0, BATCH, i // bpb), 0, 5)
    asg = pl.BlockSpec((1, 1, TOP_K * ROW_TILE), lambda i, base: (blk(i), 0, 0), memory_space=pltpu.SMEM)
    return pl.pallas_call(
        functools.partial(_combine_kernel, final),
        grid_spec=pltpu.PrefetchScalarGridSpec(
            num_scalar_prefetch=1,
            grid=(n_out,),
            in_specs=[
                asg, asg,
                pl.BlockSpec((ROW_TILE, D_MODEL), lambda i, base: (blk(i), 0)),
                pl.BlockSpec((ROW_TILE, TOP_K), lambda i, base: (blk(i), 0)),
                pl.BlockSpec((1, 1, D_MODEL), g2_index),
                pl.BlockSpec((1, D_MODEL), lambda i, base: (0, 0)),
                pl.BlockSpec(memory_space=pl.ANY),
            ],
            out_specs=pl.BlockSpec((ROW_TILE, D_MODEL), lambda i, base: (i, 0)),
            scratch_shapes=[pltpu.VMEM((TOP_K, ROW_TILE * TOKEN_SUBLANES, LANES), F32),
                            pltpu.SemaphoreType.DMA(())],
        ),
        out_shape=jax.ShapeDtypeStruct((n_out * ROW_TILE, D_MODEL), F32),
        compiler_params=_params("arbitrary"),
        name="moe_combine_final" if final else "moe_combine",
    )(slot_base, e_blk, rank_blk, xs, weights, mod_l, final_g.reshape(1, D_MODEL), ypad)


def _moe(h_tiles, logits, xs, mod_l, w1, w3, w2, layer, final_g, final):
    slots, weights, block_e, n_used = _route(logits)
    xpad = _dispatch(h_tiles, slots)
    ypad = _experts(xpad, block_e, n_used, w1, w3, w2, layer)
    return _combine(ypad, slots, weights, xs, mod_l, final_g, final)


def kernel(x, c, ctx, c_ctx, ada_w, ada_b, norm_mix_g, norm_ffn_g, even_w_in, even_w_out, a_q_gain, a_k_gain, pool_w, pool_scale, odd_w_in, odd_w_out, na_rel_bias, moe_w_group, moe_b_group, moe_w_expert, moe_b_expert, moe_w1, moe_w3, moe_w2, final_g):
    r = CTX_LEN + SEQ
    xs = jnp.concatenate([ctx, x], axis=1).reshape(BATCH * r, D_MODEL)
    cvec = jnp.concatenate([c, c_ctx[None, :], jnp.zeros((MOD_ROWS - BATCH - 1, D_MODEL), F32)], axis=0)
    mod = _modulation(cvec, ada_w, ada_b)
    for l in range(DEPTH):
        mod_l = mod[l].reshape(MOD_ROWS, 1, N_MOD * D_MODEL)
        i = l // 2
        if l % 2 == 0:
            q, k, v, u = _in_even(xs, mod_l, norm_mix_g[l], even_w_in[i], a_q_gain[i], a_k_gain[i])
            lats = [_gqa(q, k, v), _pool(u, pool_w[i], pool_scale[i])]
            w_out = even_w_out[i]
        else:
            q, k, v = _in_odd(xs, mod_l, norm_mix_g[l], odd_w_in[i])
            lats = [_na(q, k, v, na_rel_bias[i])]
            w_out = odd_w_out[i]
        wr, br = _router_weights(moe_w_group[l], moe_b_group[l], moe_w_expert[l], moe_b_expert[l])
        xs, h, logits = _out_proj(lats, w_out, xs, mod_l, norm_ffn_g[l], wr, br)
        xs = _moe(h, logits, xs, mod_l, moe_w1, moe_w3, moe_w2, l, final_g, l == DEPTH - 1)
    return xs.reshape(BATCH, SEQ, D_MODEL)
```

```python
import functools

import jax
import jax.numpy as jnp
import numpy as np
from jax import lax
from jax.experimental import pallas as pl
from jax.experimental.pallas import tpu as pltpu

D_MODEL = 1024
BATCH = 8
SEQ = 4096
DEPTH = 2
GRID_W = 64
CTX_LEN = 256
EPS = 1e-6
N_MOD = 6

A_HEADS = 8
A_KV_HEADS = 2
A_HEAD_DIM = 64
A_WIDTH = A_HEADS * A_HEAD_DIM
A_KV_WIDTH = A_KV_HEADS * A_HEAD_DIM
A_SCALE = A_HEAD_DIM ** -0.5
ROPE_THETA = 10000.0
ROPE_FREQS = A_HEAD_DIM // 4

B_GROUPS = 4
B_WIDTH = D_MODEL // 2
B_GROUP_DIM = B_WIDTH // B_GROUPS
POOL_WINDOWS = (2, 4, 8, 16)
POOL_PAD = 16

C_HEADS = 16
C_HEAD_DIM = D_MODEL // C_HEADS
C_WIDTH = C_HEADS * C_HEAD_DIM
C_SCALE = C_HEAD_DIM ** -0.5
NA_ROWS_MAX = 8
NA_COLS = 16
NA_BAND = 4
LOG2_E = 1.4426950408889634

N_GROUPS = 4
E_PER_GROUP = 8
N_EXPERTS = N_GROUPS * E_PER_GROUP
TOP_K = 2
D_EXPERT = D_MODEL // 2

LANES = 128
TOKEN_SUBLANES = D_MODEL // LANES
ROW_TILE = 256
MOE_ROWS = 256
DMA_UNROLL = 8
MOD_ROWS = 16
MOD_TILE_N = 1536
ROUTE_LANES = 128
ROUTE_EXPERT_ROW = 8
VMEM_LIMIT_BYTES = 48 * 1024 * 1024
MASKED = -1e30

F32 = jnp.float32
BF16 = jnp.bfloat16


def _params(*semantics):
    return pltpu.CompilerParams(dimension_semantics=semantics, vmem_limit_bytes=VMEM_LIMIT_BYTES)


def _split_bf16(a):
    hi = a.astype(BF16)
    lo = (a - hi.astype(F32)).astype(BF16)
    return hi, lo


def _dot(a, b):
    return jnp.dot(a, b, preferred_element_type=F32)


def _dot_nt(a, b):
    return lax.dot_general(a, b, (((1,), (1,)), ((), ())), preferred_element_type=F32)


def _silu(a):
    return a * jax.nn.sigmoid(a)


def _mod_kernel(c_ref, w_ref, b_ref, o_ref):
    s_hi, s_lo = _split_bf16(_silu(c_ref[...]))
    w_hi, w_lo = _split_bf16(w_ref[0])
    o_ref[0] = _dot(s_hi, w_hi) + _dot(s_hi, w_lo) + _dot(s_lo, w_hi) + b_ref[0]


def _modulation(cvec, ada_w, ada_b):
    depth, d, n = ada_w.shape
    return pl.pallas_call(
        _mod_kernel,
        grid=(depth, n // MOD_TILE_N),
        in_specs=[
            pl.BlockSpec((MOD_ROWS, d), lambda l, j: (0, 0)),
            pl.BlockSpec((1, d, MOD_TILE_N), lambda l, j: (l, 0, j)),
            pl.BlockSpec((1, 1, MOD_TILE_N), lambda l, j: (l, 0, j)),
        ],
        out_specs=pl.BlockSpec((1, MOD_ROWS, MOD_TILE_N), lambda l, j: (l, 0, j)),
        out_shape=jax.ShapeDtypeStruct((depth, MOD_ROWS, n), F32),
        compiler_params=_params("parallel", "parallel"),
        name="adaln_mod",
    )(cvec, ada_w, ada_b.reshape(depth, 1, n))


def _mod_spec(chunk, blocks_per_batch, ctx_row):
    def index(i):
        return (jnp.where(i % blocks_per_batch == 0, ctx_row, i // blocks_per_batch), 0, chunk)
    return pl.BlockSpec((1, 1, D_MODEL), index)


def _store_token_tiles(ref, rows):
    n = rows.shape[0]
    for s in range(TOKEN_SUBLANES):
        ref[pl.ds(s, n, stride=TOKEN_SUBLANES), :] = rows[:, s * LANES:(s + 1) * LANES]


def _load_token_tiles(ref, n, s):
    return ref[pl.ds(s, n, stride=TOKEN_SUBLANES), :]


def _norm_modulate(x, g, shift, scale):
    ms = jnp.mean(x * x, axis=-1, keepdims=True)
    return (x * lax.rsqrt(ms + EPS) * g) * (1.0 + scale) + shift


def _qk_norm_rope(a, gmat_ref, perm_ref, cos_ref, sin_ref):
    w = a.shape[-1]
    sq_hi, sq_lo = _split_bf16(a * a)
    gmat = gmat_ref[:w, :w]
    msq = _dot(sq_hi, gmat) + _dot(sq_lo, gmat)
    rot = _dot(a.astype(BF16), perm_ref[:w, :w])
    return lax.rsqrt(msq + EPS) * (a * cos_ref[...] + rot * sin_ref[...])


def _in_even_kernel(x_ref, g_ref, sh_ref, sc_ref, w_ref, cq_ref, sq_ref, ck_ref, sk_ref, gmat_ref, perm_ref,
                    q_ref, k_ref, v_ref, u_ref):
    hb = _norm_modulate(x_ref[...], g_ref[...], sh_ref[0], sc_ref[0]).astype(BF16)
    c0, c1, c2 = A_WIDTH, A_WIDTH + A_KV_WIDTH, A_WIDTH + 2 * A_KV_WIDTH
    q = _dot(hb, w_ref[:, :c0])
    q_ref[...] = _qk_norm_rope(q, gmat_ref, perm_ref, cq_ref, sq_ref).astype(BF16)
    k = _qk_norm_rope(_dot(hb, w_ref[:, c0:c1]), gmat_ref, perm_ref, ck_ref, sk_ref)
    v = _dot(hb, w_ref[:, c1:c2])
    ones_col = (lax.broadcasted_iota(jnp.int32, (x_ref.shape[0], LANES - A_HEAD_DIM), 1) == 0).astype(F32)
    for h in range(A_KV_HEADS):
        lanes = slice(h * A_HEAD_DIM, (h + 1) * A_HEAD_DIM)
        k_ref[h] = k[:, lanes].astype(BF16)
        v_ref[h] = jnp.concatenate([v[:, lanes], ones_col], axis=-1).astype(BF16)
    u_ref[...] = _dot(hb, w_ref[:, c2:])


def _rope_tables(q_gain, k_gain):
    t = np.arange(SEQ)
    pos = np.stack([t // GRID_W, t % GRID_W], axis=-1).astype(np.float32)
    inv = (ROPE_THETA ** (-np.arange(ROPE_FREQS, dtype=np.float32) / ROPE_FREQS)).astype(np.float32)
    ang = pos[:, :, None] * inv
    cos = np.concatenate([np.cos(ang), np.cos(ang)], axis=-1).reshape(SEQ, A_HEAD_DIM)
    sin = np.concatenate([-np.sin(ang), np.sin(ang)], axis=-1).reshape(SEQ, A_HEAD_DIM)
    cos = np.concatenate([np.ones((CTX_LEN, A_HEAD_DIM), np.float32), cos.astype(np.float32)], axis=0)
    sin = np.concatenate([np.zeros((CTX_LEN, A_HEAD_DIM), np.float32), sin.astype(np.float32)], axis=0)
    swap = _rope_swap()
    def tables(gain, heads, scale):
        c = jnp.asarray(cos) * (gain * scale)[None, :]
        s = jnp.asarray(sin) * (gain[swap] * scale)[None, :]
        return jnp.tile(c, (1, heads)), jnp.tile(s, (1, heads))
    cq, sq = tables(q_gain, A_HEADS, A_SCALE * LOG2_E)
    ck, sk = tables(k_gain, A_KV_HEADS, 1.0)
    return cq, sq, ck, sk


def _rope_swap():
    j = np.arange(A_HEAD_DIM)
    half = (j // ROPE_FREQS) % 2
    return np.where(half == 0, j + ROPE_FREQS, j - ROPE_FREQS)


def _head_matrices():
    lane = np.arange(A_WIDTH)
    same_head = (lane[:, None] // A_HEAD_DIM) == (lane[None, :] // A_HEAD_DIM)
    gmat = same_head.astype(np.float32) / A_HEAD_DIM
    swap = np.tile(_rope_swap(), A_HEADS) + (lane // A_HEAD_DIM) * A_HEAD_DIM
    perm = np.zeros((A_WIDTH, A_WIDTH), np.float32)
    perm[swap, lane] = 1.0
    return jnp.asarray(gmat, BF16), jnp.asarray(perm, BF16)


def _in_even(xs, mod_l, norm_g, w_in, q_gain, k_gain):
    t_rows = xs.shape[0]
    bpb = (CTX_LEN + SEQ) // ROW_TILE
    cq, sq, ck, sk = _rope_tables(q_gain, k_gain)
    gmat, perm = _head_matrices()
    n_in = w_in.shape[1]
    row = lambda w: pl.BlockSpec((ROW_TILE, w), lambda i: (i, 0))
    tab = lambda w: pl.BlockSpec((ROW_TILE, w), lambda i: (i % bpb, 0))
    full = lambda a, b: pl.BlockSpec((a, b), lambda i: (0, 0))
    return pl.pallas_call(
        _in_even_kernel,
        grid=(t_rows // ROW_TILE,),
        in_specs=[
            row(D_MODEL), full(1, D_MODEL), _mod_spec(0, bpb, BATCH), _mod_spec(1, bpb, BATCH),
            full(D_MODEL, n_in), tab(A_WIDTH), tab(A_WIDTH), tab(A_KV_WIDTH), tab(A_KV_WIDTH),
            full(A_WIDTH, A_WIDTH), full(A_WIDTH, A_WIDTH),
        ],
        out_specs=[row(A_WIDTH),
                   pl.BlockSpec((A_KV_HEADS, ROW_TILE, A_HEAD_DIM), lambda i: (0, i, 0)),
                   pl.BlockSpec((A_KV_HEADS, ROW_TILE, LANES), lambda i: (0, i, 0)),
                   row(B_WIDTH)],
        out_shape=[
            jax.ShapeDtypeStruct((t_rows, A_WIDTH), BF16),
            jax.ShapeDtypeStruct((A_KV_HEADS, t_rows, A_HEAD_DIM), BF16),
            jax.ShapeDtypeStruct((A_KV_HEADS, t_rows, LANES), BF16),
            jax.ShapeDtypeStruct((t_rows, B_WIDTH), F32),
        ],
        compiler_params=_params("parallel"),
        name="even_in_proj",
    )(xs, norm_g.reshape(1, D_MODEL), mod_l, mod_l, w_in.astype(BF16), cq, sq, ck, sk, gmat, perm)


def _gqa_kernel(q_ref, k_ref, v_ref, o_ref):
    def attend(n_keys):
        k = k_ref[0, :n_keys, :]
        v = v_ref[0, :n_keys, :]
        outs = []
        for h in range(A_HEADS // A_KV_HEADS):
            s = _dot_nt(q_ref[:, h * A_HEAD_DIM:(h + 1) * A_HEAD_DIM], k)
            m = jnp.max(s, axis=-1, keepdims=True)
            p = jnp.exp2(s - m).astype(BF16)
            acc = _dot(p, v)
            outs.append(acc[:, :A_HEAD_DIM] / acc[:, A_HEAD_DIM:A_HEAD_DIM + 1])
        o_ref[...] = jnp.concatenate(outs, axis=-1).astype(BF16)

    is_ctx = pl.program_id(1) == 0

    @pl.when(is_ctx)
    def _():
        attend(CTX_LEN)

    @pl.when(jnp.logical_not(is_ctx))
    def _():
        attend(CTX_LEN + SEQ)


def _gqa(q, k_heads, v_heads):
    t_rows = q.shape[0]
    r = CTX_LEN + SEQ
    bpb = r // ROW_TILE
    group_w = (A_HEADS // A_KV_HEADS) * A_HEAD_DIM
    return pl.pallas_call(
        _gqa_kernel,
        grid=(BATCH, bpb, A_KV_HEADS),
        in_specs=[
            pl.BlockSpec((ROW_TILE, group_w), lambda b, j, g: (b * bpb + j, g)),
            pl.BlockSpec((1, r, A_HEAD_DIM), lambda b, j, g: (g, b, 0)),
            pl.BlockSpec((1, r, LANES), lambda b, j, g: (g, b, 0)),
        ],
        out_specs=pl.BlockSpec((ROW_TILE, group_w), lambda b, j, g: (b * bpb + j, g)),
        out_shape=jax.ShapeDtypeStruct((t_rows, A_WIDTH), BF16),
        compiler_params=_params("parallel", "parallel", "parallel"),
        name="gqa_attention",
    )(q, k_heads, v_heads)


def _pool_kernel(u_ref, w_ref, s_ref, o_ref, pad_ref):
    g = pl.program_id(1)
    w_bf = w_ref[0].astype(BF16)
    for start, length in ((0, CTX_LEN), (CTX_LEN, SEQ)):
        u = u_ref[start:start + length, :]
        pad_ref[0:POOL_PAD, :] = jnp.zeros((POOL_PAD, LANES), F32)
        pad_ref[POOL_PAD:POOL_PAD + length, :] = u
        pad_ref[POOL_PAD + length:2 * POOL_PAD + length, :] = jnp.zeros((POOL_PAD, LANES), F32)
        t = lax.broadcasted_iota(jnp.int32, (length, LANES), 0)
        for gi, win in enumerate(POOL_WINDOWS):
            @pl.when(g == gi)
            def _(win=win, u=u, t=t, start=start, length=length):
                base = POOL_PAD - win // 2
                acc = pad_ref[base:base + length, :]
                for j in range(1, win):
                    acc = acc + pad_ref[base + j:base + j + length, :]
                lo = jnp.clip(t - win // 2, 0, length)
                hi = jnp.clip(t - win // 2 + win, 0, length)
                pooled = acc / (hi - lo).astype(F32) - u
                mixed = _dot(pooled.astype(BF16), w_bf) * s_ref[0]
                o_ref[start:start + length, :] = mixed.astype(BF16)


def _pool(u, pool_w, pool_scale):
    t_rows = u.shape[0]
    r = CTX_LEN + SEQ
    return pl.pallas_call(
        _pool_kernel,
        grid=(BATCH, B_GROUPS),
        in_specs=[
            pl.BlockSpec((r, B_GROUP_DIM), lambda b, g: (b, g)),
            pl.BlockSpec((1, B_GROUP_DIM, B_GROUP_DIM), lambda b, g: (g, 0, 0)),
            pl.BlockSpec((1, 1, B_GROUP_DIM), lambda b, g: (g, 0, 0)),
        ],
        out_specs=pl.BlockSpec((r, B_GROUP_DIM), lambda b, g: (b, g)),
        out_shape=jax.ShapeDtypeStruct((t_rows, B_WIDTH), BF16),
        scratch_shapes=[pltpu.VMEM((SEQ + 2 * POOL_PAD, B_GROUP_DIM), F32)],
        compiler_params=_params("parallel", "parallel"),
        name="pool_mixer",
    )(u, pool_w, pool_scale.reshape(B_GROUPS, 1, B_GROUP_DIM))


def _in_odd_kernel(x_ref, g_ref, sh_ref, sc_ref, w_ref, q_ref, k_ref, v_ref):
    hb = _norm_modulate(x_ref[...], g_ref[...], sh_ref[0], sc_ref[0]).astype(BF16)
    q_ref[...] = (_dot(hb, w_ref[:, :C_WIDTH]) * (C_SCALE * LOG2_E)).astype(BF16)
    k_ref[...] = _dot(hb, w_ref[:, C_WIDTH:2 * C_WIDTH]).astype(BF16)
    v_ref[...] = _dot(hb, w_ref[:, 2 * C_WIDTH:]).astype(BF16)


def _in_odd(xs, mod_l, norm_g, w_in):
    t_rows = xs.shape[0]
    bpb = (CTX_LEN + SEQ) // ROW_TILE
    row = lambda w: pl.BlockSpec((ROW_TILE, w), lambda i: (i, 0))
    full = lambda a, b: pl.BlockSpec((a, b), lambda i: (0, 0))
    return pl.pallas_call(
        _in_odd_kernel,
        grid=(t_rows // ROW_TILE,),
        in_specs=[row(D_MODEL), full(1, D_MODEL), _mod_spec(0, bpb, BATCH), _mod_spec(1, bpb, BATCH),
                  full(D_MODEL, 3 * C_WIDTH)],
        out_specs=[row(C_WIDTH)] * 3,
        out_shape=[jax.ShapeDtypeStruct((t_rows, C_WIDTH), BF16)] * 3,
        compiler_params=_params("parallel"),
        name="odd_in_proj",
    )(xs, norm_g.reshape(1, D_MODEL), mod_l, mod_l, w_in.astype(BF16))


def _na_plan():
    rows = SEQ // GRID_W
    kh = min(NA_ROWS_MAX, rows)
    win_rows = min(kh + NA_BAND - 1, rows)
    starts, types, patterns = [], [], []
    for r0 in range(0, rows, NA_BAND):
        ws = int(np.clip(r0 - kh // 2, 0, rows - win_rows))
        pat = np.full((NA_BAND, win_rows), -1, np.int64)
        for rr in range(NA_BAND):
            r = r0 + rr
            rs = int(np.clip(r - kh // 2, 0, rows - kh))
            for wi in range(win_rows):
                if rs <= ws + wi < rs + kh:
                    pat[rr, wi] = ws + wi - r + (NA_ROWS_MAX - 1)
        for t, known in enumerate(patterns):
            if np.array_equal(known, pat):
                types.append(t)
                break
        else:
            types.append(len(patterns))
            patterns.append(pat)
        starts.append(ws)
    return win_rows, np.asarray(starts, np.int32), np.asarray(types, np.int32), patterns


def _na_bias_table(rel_bias, patterns):
    n_off = 2 * NA_ROWS_MAX - 1
    cq = np.arange(GRID_W)
    col_start = np.clip(cq - NA_COLS // 2, 0, GRID_W - NA_COLS)
    ck = np.arange(GRID_W)
    inside = (ck[None, :] >= col_start[:, None]) & (ck[None, :] < col_start[:, None] + NA_COLS)
    shift = GRID_W - NA_COLS
    padded = jnp.pad(rel_bias.astype(F32) * LOG2_E, ((0, 0), (0, 0), (shift, shift)))
    tcol = jnp.stack([padded[:, :, GRID_W - 1 - c:2 * GRID_W - 1 - c] for c in range(GRID_W)], axis=2)
    tcol = jnp.where(jnp.asarray(inside)[None, None], tcol, MASKED)
    tcol = jnp.concatenate([tcol, jnp.full((C_HEADS, 1, GRID_W, GRID_W), MASKED, F32)], axis=1)
    tables = []
    for pat in patterns:
        idx = np.where(pat < 0, n_off, pat)
        blocks = tcol[:, idx].transpose(0, 1, 3, 2, 4)
        tables.append(blocks.reshape(C_HEADS, idx.shape[0] * GRID_W, idx.shape[1] * GRID_W))
    return jnp.stack(tables, axis=1)


def _na_kernel(ws_ref, ty_ref, q_ref, k_ref, v_ref, bias_ref, o_ref):
    n_bands = (SEQ // GRID_W) // NA_BAND
    band_q = NA_BAND * GRID_W
    win = bias_ref.shape[-1]
    o_ref[0:CTX_LEN, :] = jnp.zeros((CTX_LEN, LANES), BF16)
    kc = k_ref[0:CTX_LEN, :]
    vc = v_ref[0:CTX_LEN, :]
    first = lax.broadcasted_iota(jnp.int32, (1, LANES), 1) < C_HEAD_DIM

    def one_band(b, carry):
        q_off = pl.multiple_of(CTX_LEN + b * band_q, GRID_W)
        k_off = pl.multiple_of(CTX_LEN + ws_ref[b] * GRID_W, GRID_W)
        qb = q_ref[pl.ds(q_off, band_q), :]
        kw = k_ref[pl.ds(k_off, win), :]
        vw = v_ref[pl.ds(k_off, win), :]
        outs = []
        for h in range(2):
            qm = jnp.where(first if h == 0 else jnp.logical_not(first), qb, jnp.zeros_like(qb))
            s_loc = _dot_nt(qm, kw) + bias_ref[h, ty_ref[b]]
            s_ctx = _dot_nt(qm, kc)
            m = jnp.maximum(jnp.max(s_loc, axis=-1, keepdims=True), jnp.max(s_ctx, axis=-1, keepdims=True))
            p_loc = jnp.exp2(s_loc - m)
            p_ctx = jnp.exp2(s_ctx - m)
            denom = jnp.sum(p_loc, axis=-1, keepdims=True) + jnp.sum(p_ctx, axis=-1, keepdims=True)
            o = _dot(p_loc.astype(BF16), vw) + _dot(p_ctx.astype(BF16), vc)
            outs.append(o / denom)
        o_ref[pl.ds(q_off, band_q), :] = jnp.where(first, outs[0], outs[1]).astype(BF16)
        return carry

    lax.fori_loop(0, n_bands, one_band, 0, unroll=2)


def _na(q, k, v, rel_bias):
    t_rows = q.shape[0]
    r = CTX_LEN + SEQ
    win_rows, starts, types, patterns = _na_plan()
    bias = _na_bias_table(rel_bias, patterns)
    blk = pl.BlockSpec((r, LANES), lambda p, b, ws, ty: (b, p))
    return pl.pallas_call(
        _na_kernel,
        grid_spec=pltpu.PrefetchScalarGridSpec(
            num_scalar_prefetch=2,
            grid=(C_HEADS // 2, BATCH),
            in_specs=[blk, blk, blk,
                      pl.BlockSpec((2, len(patterns), NA_BAND * GRID_W, win_rows * GRID_W),
                                   lambda p, b, ws, ty: (p, 0, 0, 0))],
            out_specs=blk,
        ),
        out_shape=jax.ShapeDtypeStruct((t_rows, C_WIDTH), BF16),
        compiler_params=_params("parallel", "parallel"),
        name="neighbourhood_attention",
    )(jnp.asarray(starts), jnp.asarray(types), q, k, v, bias)


def _out_kernel(n_lat, *refs):
    lat_refs = refs[:n_lat]
    w_ref, x_ref, g1_ref, g_ref, sh_ref, sc_ref, wr_ref, br_ref, xo_ref, h_ref, lg_ref = refs[n_lat:]
    y = None
    off = 0
    for lat in lat_refs:
        width = lat.shape[-1]
        part = _dot(lat[...], w_ref[off:off + width, :])
        y = part if y is None else y + part
        off += width
    x_new = x_ref[...] + g1_ref[0] * y
    xo_ref[...] = x_new
    h = _norm_modulate(x_new, g_ref[...], sh_ref[0], sc_ref[0])
    _store_token_tiles(h_ref, h)
    h_hi, h_lo = _split_bf16(h)
    both = _dot(h_hi, wr_ref[...])
    lg_ref[...] = (both[:, :ROUTE_LANES] + both[:, ROUTE_LANES:]
                   + _dot(h_lo, wr_ref[:, :ROUTE_LANES]) + br_ref[...])


def _router_weights(wg, bg, we, be):
    gap = ROUTE_EXPERT_ROW - N_GROUPS
    tail = ROUTE_LANES - ROUTE_EXPERT_ROW - N_EXPERTS
    d = wg.shape[0]
    w = jnp.concatenate([wg, jnp.zeros((d, gap), F32), we, jnp.zeros((d, tail), F32)], axis=1)
    w_hi, w_lo = _split_bf16(w)
    b = jnp.concatenate([bg, jnp.zeros((gap,), F32), be, jnp.zeros((tail,), F32)])
    return jnp.concatenate([w_hi, w_lo], axis=1), b.reshape(1, ROUTE_LANES)


def _out_proj(lats, w_out, xs, mod_l, norm_g, wr, br):
    t_rows = xs.shape[0]
    bpb = (CTX_LEN + SEQ) // ROW_TILE
    row = lambda w: pl.BlockSpec((ROW_TILE, w), lambda i: (i, 0))
    full = lambda a, b: pl.BlockSpec((a, b), lambda i: (0, 0))
    return pl.pallas_call(
        functools.partial(_out_kernel, len(lats)),
        grid=(t_rows // ROW_TILE,),
        in_specs=[row(l.shape[1]) for l in lats] + [
            full(*w_out.shape), row(D_MODEL), _mod_spec(2, bpb, BATCH), full(1, D_MODEL),
            _mod_spec(3, bpb, BATCH), _mod_spec(4, bpb, BATCH), full(D_MODEL, 2 * ROUTE_LANES), full(1, ROUTE_LANES),
        ],
        out_specs=[row(D_MODEL), pl.BlockSpec((ROW_TILE * TOKEN_SUBLANES, LANES), lambda i: (i, 0)),
                   row(ROUTE_LANES)],
        out_shape=[
            jax.ShapeDtypeStruct((t_rows, D_MODEL), F32),
            jax.ShapeDtypeStruct((t_rows * TOKEN_SUBLANES, LANES), F32),
            jax.ShapeDtypeStruct((t_rows, ROUTE_LANES), F32),
        ],
        compiler_params=_params("parallel"),
        name="out_proj_ffn_norm",
    )(*lats, w_out.astype(BF16), xs, mod_l, norm_g.reshape(1, D_MODEL), mod_l, mod_l, wr, br)


def _route_kernel(lg_ref, tri_ref, e_ref, rk_ref, wt_ref, cnt_ref, run_ref):
    @pl.when(pl.program_id(0) == 0)
    def _():
        run_ref[...] = jnp.zeros(run_ref.shape, F32)

    lt = lg_ref[...].T
    sub = lax.broadcasted_iota(jnp.int32, (E_PER_GROUP, ROW_TILE), 0)
    gl = jnp.where(sub < N_GROUPS, lt[0:E_PER_GROUP], -jnp.inf)
    g_max = jnp.max(gl, axis=0, keepdims=True)
    g_idx = jnp.min(jnp.where(gl == g_max, sub, E_PER_GROUP), axis=0, keepdims=True)
    g_w = 1.0 / jnp.sum(jnp.exp(gl - g_max), axis=0, keepdims=True)
    sel = lt[ROUTE_EXPERT_ROW:ROUTE_EXPERT_ROW + E_PER_GROUP]
    for g in range(1, N_GROUPS):
        lo = ROUTE_EXPERT_ROW + g * E_PER_GROUP
        sel = jnp.where(g_idx == g, lt[lo:lo + E_PER_GROUP], sel)
    m1 = jnp.max(sel, axis=0, keepdims=True)
    i1 = jnp.min(jnp.where(sel == m1, sub, E_PER_GROUP), axis=0, keepdims=True)
    rest = jnp.where(sub == i1, -jnp.inf, sel)
    m2 = jnp.max(rest, axis=0, keepdims=True)
    i2 = jnp.min(jnp.where(rest == m2, sub, E_PER_GROUP), axis=0, keepdims=True)
    t = jnp.exp(m2 - m1)
    p1 = 1.0 / (1.0 + t)
    wt_ref[0] = jnp.concatenate([g_w * p1, g_w * (t * p1)], axis=0)
    e_flat = jnp.concatenate([g_idx * E_PER_GROUP + i1, g_idx * E_PER_GROUP + i2], axis=1)
    e_ref[0] = e_flat
    hit = lax.broadcasted_iota(jnp.int32, (N_EXPERTS, TOP_K * ROW_TILE), 0) == e_flat
    onehot = jnp.where(hit, 1.0, 0.0)
    before = _dot(onehot.astype(BF16), tri_ref[...]) + run_ref[:, 0:1]
    rk_ref[0] = jnp.sum(jnp.where(hit, before, 0.0), axis=0, keepdims=True).astype(jnp.int32)
    run_ref[...] = run_ref[...] + jnp.sum(onehot, axis=1, keepdims=True)
    cnt_ref[...] = run_ref[...]


def _route(logits):
    t_rows = logits.shape[0]
    n_blk = t_rows // ROW_TILE
    n_asg = TOP_K * ROW_TILE
    tri = jnp.asarray(np.triu(np.ones((n_asg, n_asg), np.float32), 1), BF16)
    e_blk, rank_blk, wt_blk, counts = pl.pallas_call(
        _route_kernel,
        grid=(n_blk,),
        in_specs=[pl.BlockSpec((ROW_TILE, ROUTE_LANES), lambda i: (i, 0)),
                  pl.BlockSpec((n_asg, n_asg), lambda i: (0, 0))],
        out_specs=[pl.BlockSpec((1, 1, n_asg), lambda i: (i, 0, 0)),
                   pl.BlockSpec((1, 1, n_asg), lambda i: (i, 0, 0)),
                   pl.BlockSpec((1, TOP_K, ROW_TILE), lambda i: (i, 0, 0)),
                   pl.BlockSpec((N_EXPERTS, LANES), lambda i: (0, 0))],
        out_shape=[jax.ShapeDtypeStruct((n_blk, 1, n_asg), jnp.int32),
                   jax.ShapeDtypeStruct((n_blk, 1, n_asg), jnp.int32),
                   jax.ShapeDtypeStruct((n_blk, TOP_K, ROW_TILE), F32),
                   jax.ShapeDtypeStruct((N_EXPERTS, LANES), F32)],
        scratch_shapes=[pltpu.VMEM((N_EXPERTS, LANES), F32)],
        compiler_params=_params("arbitrary"),
        name="moe_route",
    )(logits, tri)
    counts = counts[:, 0].astype(jnp.int32)
    blocks_of = (counts + MOE_ROWS - 1) // MOE_ROWS
    b_end = jnp.cumsum(blocks_of)
    slot_base = ((b_end - blocks_of) * MOE_ROWS).astype(jnp.int32)
    n_used = b_end[-1].astype(jnp.int32)
    blk = jnp.minimum(jnp.arange(_moe_blocks(t_rows), dtype=jnp.int32), n_used - 1)
    block_e = jnp.minimum(jnp.sum((b_end[None, :] <= blk[:, None]).astype(jnp.int32), axis=1), N_EXPERTS - 1)
    weights = wt_blk.transpose(0, 2, 1).reshape(t_rows, TOP_K)
    return (slot_base, counts, e_blk, rank_blk), weights, block_e, n_used.reshape(1)


def _moe_blocks(t_rows):
    return (t_rows * TOP_K) // MOE_ROWS + N_EXPERTS


def _tile_copy(src_ref, src_token, dst_ref, dst_token, sem):
    src = src_ref.at[pl.ds(pl.multiple_of(src_token * TOKEN_SUBLANES, TOKEN_SUBLANES), TOKEN_SUBLANES)]
    dst = dst_ref.at[pl.ds(pl.multiple_of(dst_token * TOKEN_SUBLANES, TOKEN_SUBLANES), TOKEN_SUBLANES)]
    return pltpu.make_async_copy(src, dst, sem)


def _start_then_wait(copies):
    def start(r, c):
        for n, cp in enumerate(copies(r)):
            cp.start(priority=n % 2)
        return c

    def wait(r, c):
        for cp in copies(r):
            cp.wait()
        return c

    lax.fori_loop(0, ROW_TILE, start, 0, unroll=DMA_UNROLL)
    lax.fori_loop(0, ROW_TILE, wait, 0, unroll=DMA_UNROLL)


def _slot(base_ref, e_ref, rk_ref, j):
    return base_ref[e_ref[0, 0, j]] + rk_ref[0, 0, j]


def _dispatch_kernel(base_ref, cnt_ref, e_ref, rk_ref, h_ref, xpad_ref, zero_ref, sem):
    _start_then_wait(lambda r: [
        _tile_copy(h_ref, r, xpad_ref, _slot(base_ref, e_ref, rk_ref, k * ROW_TILE + r), sem)
        for k in range(TOP_K)])

    @pl.when(pl.program_id(0) == pl.num_programs(0) - 1)
    def _():
        zero_ref[...] = jnp.zeros(zero_ref.shape, F32)
        block_end = lambda e: base_ref[e] + (cnt_ref[e] + MOE_ROWS - 1) // MOE_ROWS * MOE_ROWS

        def pad_expert(e, c):
            lo = base_ref[e] + cnt_ref[e]
            zero_copy = lambda j: _tile_copy(zero_ref, 0, xpad_ref, j, sem)
            lax.fori_loop(lo, block_end(e), lambda j, cc: (zero_copy(j).start(), cc)[1], 0)
            lax.fori_loop(lo, block_end(e), lambda j, cc: (zero_copy(j).wait(), cc)[1], 0)
            return c

        lax.fori_loop(0, N_EXPERTS, pad_expert, 0)
        block_rows = MOE_ROWS * TOKEN_SUBLANES
        first_unused = block_end(N_EXPERTS - 1) // MOE_ROWS
        n_blocks = xpad_ref.shape[0] // block_rows
        block_copy = lambda b: pltpu.make_async_copy(
            zero_ref, xpad_ref.at[pl.ds(pl.multiple_of(b * block_rows, block_rows), block_rows)], sem)
        lax.fori_loop(first_unused, n_blocks, lambda b, cc: (block_copy(b).start(), cc)[1], 0)
        lax.fori_loop(first_unused, n_blocks, lambda b, cc: (block_copy(b).wait(), cc)[1], 0)


def _dispatch(h_tiles, slots):
    slot_base, counts, e_blk, rank_blk = slots
    t_rows = h_tiles.shape[0] // TOKEN_SUBLANES
    n_pad = _moe_blocks(t_rows) * MOE_ROWS
    asg = pl.BlockSpec((1, 1, TOP_K * ROW_TILE), lambda i, base, cnt: (i, 0, 0), memory_space=pltpu.SMEM)
    return pl.pallas_call(
        _dispatch_kernel,
        grid_spec=pltpu.PrefetchScalarGridSpec(
            num_scalar_prefetch=2,
            grid=(t_rows // ROW_TILE,),
            in_specs=[asg, asg,
                      pl.BlockSpec((ROW_TILE * TOKEN_SUBLANES, LANES), lambda i, base, cnt: (i, 0))],
            out_specs=pl.BlockSpec(memory_space=pl.ANY),
            scratch_shapes=[pltpu.VMEM((MOE_ROWS * TOKEN_SUBLANES, LANES), F32), pltpu.SemaphoreType.DMA(())],
        ),
        out_shape=jax.ShapeDtypeStruct((n_pad * TOKEN_SUBLANES, LANES), F32),
        compiler_params=_params("arbitrary"),
        name="moe_dispatch",
    )(slot_base, counts, e_blk, rank_blk, h_tiles)


def _expert_kernel(be_ref, nu_ref, x_ref, w1_ref, w3_ref, w2_ref, y_ref, w1_bf, w3_bf, w2_bf):
    i = pl.program_id(0)
    fresh = jnp.logical_or(i == 0, be_ref[i] != be_ref[jnp.maximum(i - 1, 0)])

    @pl.when(fresh)
    def _():
        w1_bf[...] = w1_ref[0, 0].astype(BF16)
        w3_bf[...] = w3_ref[0, 0].astype(BF16)
        w2_bf[...] = w2_ref[0, 0].astype(BF16)

    used = i < nu_ref[0]

    @pl.when(used)
    def _():
        xb = jnp.concatenate(
            [_load_token_tiles(x_ref, MOE_ROWS, s).astype(BF16) for s in range(TOKEN_SUBLANES)], axis=-1)
        hid = _silu(_dot(xb, w1_bf[...])) * _dot(xb, w3_bf[...])
        _store_token_tiles(y_ref, _dot(hid.astype(BF16), w2_bf[...]))

    @pl.when(jnp.logical_not(used))
    def _():
        y_ref[...] = jnp.zeros(y_ref.shape, F32)


def _experts(xpad, block_e, n_used, w1, w3, w2, layer):
    g_blocks = xpad.shape[0] // (MOE_ROWS * TOKEN_SUBLANES)
    tile_rows = MOE_ROWS * TOKEN_SUBLANES
    x_index = lambda i, be, nu: (jnp.minimum(i, nu[0] - 1), 0)
    w_index = lambda i, be, nu: (layer, be[i], 0, 0)
    return pl.pallas_call(
        _expert_kernel,
        grid_spec=pltpu.PrefetchScalarGridSpec(
            num_scalar_prefetch=2,
            grid=(g_blocks,),
            in_specs=[
                pl.BlockSpec((tile_rows, LANES), x_index),
                pl.BlockSpec((1, 1, D_MODEL, D_EXPERT), w_index),
                pl.BlockSpec((1, 1, D_MODEL, D_EXPERT), w_index),
                pl.BlockSpec((1, 1, D_EXPERT, D_MODEL), w_index),
            ],
            out_specs=pl.BlockSpec((tile_rows, LANES), lambda i, be, nu: (i, 0)),
            scratch_shapes=[
                pltpu.VMEM((D_MODEL, D_EXPERT), BF16),
                pltpu.VMEM((D_MODEL, D_EXPERT), BF16),
                pltpu.VMEM((D_EXPERT, D_MODEL), BF16),
            ],
        ),
        out_shape=jax.ShapeDtypeStruct((g_blocks * tile_rows, LANES), F32),
        compiler_params=_params("arbitrary"),
        name="moe_experts",
    )(block_e, n_used, xpad, w1, w3, w2)


def _combine_kernel(final, base_ref, e_ref, rk_ref, x_ref, wt_ref, g2_ref, fg_ref, ypad_ref, o_ref, ybuf, sem):
    _start_then_wait(lambda r: [
        _tile_copy(ypad_ref, _slot(base_ref, e_ref, rk_ref, k * ROW_TILE + r), ybuf.at[k], r, sem)
        for k in range(TOP_K)])
    wt = wt_ref[...]
    parts = []
    for s in range(TOKEN_SUBLANES):
        lanes = slice(s * LANES, (s + 1) * LANES)
        f = (wt[:, 0:1] * _load_token_tiles(ybuf.at[0], ROW_TILE, s)
             + wt[:, 1:2] * _load_token_tiles(ybuf.at[1], ROW_TILE, s))
        parts.append(x_ref[:, lanes] + g2_ref[0][:, lanes] * f)
    x_new = jnp.concatenate(parts, axis=-1)
    if final:
        ms = jnp.mean(x_new * x_new, axis=-1, keepdims=True)
        x_new = x_new * lax.rsqrt(ms + EPS) * fg_ref[...]
    o_ref[...] = x_new


def _combine(ypad, slots, weights, xs, mod_l, final_g, final):
    slot_base, _, e_blk, rank_blk = slots
    t_rows = xs.shape[0]
    bpb = (CTX_LEN + SEQ) // ROW_TILE
    if final:
        xpb = SEQ // ROW_TILE
        n_out = BATCH * xpb
        blk = lambda i: (i // xpb) * bpb + i % xpb + 1
        g2_index = lambda i, base: (i // xpb, 0, 5)
    else:
        n_out = t_rows // ROW_TILE
        blk = lambda i: i
        g2_index = lambda i, base: (jnp.where(i % bpb == 0, BATCH, i // bpb), 0, 5)
    asg = pl.BlockSpec((1, 1, TOP_K * ROW_TILE), lambda i, base: (blk(i), 0, 0), memory_space=pltpu.SMEM)
    return pl.pallas_call(
        functools.partial(_combine_kernel, final),
        grid_spec=pltpu.PrefetchScalarGridSpec(
            num_scalar_prefetch=1,
            grid=(n_out,),
            in_specs=[
                asg, asg,
                pl.BlockSpec((ROW_TILE, D_MODEL), lambda i, base: (blk(i), 0)),
                pl.BlockSpec((ROW_TILE, TOP_K), lambda i, base: (blk(i), 0)),
                pl.BlockSpec((1, 1, D_MODEL), g2_index),
                pl.BlockSpec((1, D_MODEL), lambda i, base: (0, 0)),
                pl.BlockSpec(memory_space=pl.ANY),
            ],
            out_specs=pl.BlockSpec((ROW_TILE, D_MODEL), lambda i, base: (i, 0)),
            scratch_shapes=[pltpu.VMEM((TOP_K, ROW_TILE * TOKEN_SUBLANES, LANES), F32),
                            pltpu.SemaphoreType.DMA(())],
        ),
        out_shape=jax.ShapeDtypeStruct((n_out * ROW_TILE, D_MODEL), F32),
        compiler_params=_params("arbitrary"),
        name="moe_combine_final" if final else "moe_combine",
    )(slot_base, e_blk, rank_blk, xs, weights, mod_l, final_g.reshape(1, D_MODEL), ypad)


def _moe(h_tiles, logits, xs, mod_l, w1, w3, w2, layer, final_g, final):
    slots, weights, block_e, n_used = _route(logits)
    xpad = _dispatch(h_tiles, slots)
    ypad = _experts(xpad, block_e, n_used, w1, w3, w2, layer)
    return _combine(ypad, slots, weights, xs, mod_l, final_g, final)


def kernel(x, c, ctx, c_ctx, ada_w, ada_b, norm_mix_g, norm_ffn_g, even_w_in, even_w_out, a_q_gain, a_k_gain, pool_w, pool_scale, odd_w_in, odd_w_out, na_rel_bias, moe_w_group, moe_b_group, moe_w_expert, moe_b_expert, moe_w1, moe_w3, moe_w2, final_g):
    r = CTX_LEN + SEQ
    xs = jnp.concatenate([ctx, x], axis=1).reshape(BATCH * r, D_MODEL)
    cvec = jnp.concatenate([c, c_ctx[None, :], jnp.zeros((MOD_ROWS - BATCH - 1, D_MODEL), F32)], axis=0)
    mod = _modulation(cvec, ada_w, ada_b)
    for l in range(DEPTH):
        mod_l = mod[l].reshape(MOD_ROWS, 1, N_MOD * D_MODEL)
        i = l // 2
        if l % 2 == 0:
            q, k, v, u = _in_even(xs, mod_l, norm_mix_g[l], even_w_in[i], a_q_gain[i], a_k_gain[i])
            lats = [_gqa(q, k, v), _pool(u, pool_w[i], pool_scale[i])]
            w_out = even_w_out[i]
        else:
            q, k, v = _in_odd(xs, mod_l, norm_mix_g[l], odd_w_in[i])
            lats = [_na(q, k, v, na_rel_bias[i])]
            w_out = odd_w_out[i]
        wr, br = _router_weights(moe_w_group[l], moe_b_group[l], moe_w_expert[l], moe_b_expert[l])
        xs, h, logits = _out_proj(lats, w_out, xs, mod_l, norm_ffn_g[l], wr, br)
        xs = _moe(h, logits, xs, mod_l, moe_w1, moe_w3, moe_w2, l, final_g, l == DEPTH - 1)
    return xs.reshape(BATCH, SEQ, D_MODEL)
```

```python
import functools

import jax
import jax.numpy as jnp
import numpy as np
from jax import lax
from jax.experimental import pallas as pl
from jax.experimental.pallas import tpu as pltpu

D_MODEL = 1024
BATCH = 8
SEQ = 4096
DEPTH = 2
GRID_W = 64
CTX_LEN = 256
EPS = 1e-6
N_MOD = 6

A_HEADS = 8
A_KV_HEADS = 2
A_HEAD_DIM = 64
A_WIDTH = A_HEADS * A_HEAD_DIM
A_KV_WIDTH = A_KV_HEADS * A_HEAD_DIM
A_SCALE = A_HEAD_DIM ** -0.5
ROPE_THETA = 10000.0
ROPE_FREQS = A_HEAD_DIM // 4

B_GROUPS = 4
B_WIDTH = D_MODEL // 2
B_GROUP_DIM = B_WIDTH // B_GROUPS
POOL_WINDOWS = (2, 4, 8, 16)
POOL_PAD = 16

C_HEADS = 16
C_HEAD_DIM = D_MODEL // C_HEADS
C_WIDTH = C_HEADS * C_HEAD_DIM
C_SCALE = C_HEAD_DIM ** -0.5
NA_ROWS_MAX = 8
NA_COLS = 16
NA_BAND = 4
LOG2_E = 1.4426950408889634

N_GROUPS = 4
E_PER_GROUP = 8
N_EXPERTS = N_GROUPS * E_PER_GROUP
TOP_K = 2
D_EXPERT = D_MODEL // 2

LANES = 128
TOKEN_SUBLANES = D_MODEL // LANES
ROW_TILE = 256
MOE_ROWS = 512
DMA_UNROLL = 8
MOD_ROWS = 16
MOD_TILE_N = 1536
ROUTE_LANES = 128
ROUTE_EXPERT_ROW = 8
VMEM_LIMIT_BYTES = 48 * 1024 * 1024
MASKED = -1e30

F32 = jnp.float32
BF16 = jnp.bfloat16


def _params(*semantics):
    return pltpu.CompilerParams(dimension_semantics=semantics, vmem_limit_bytes=VMEM_LIMIT_BYTES)


def _split_bf16(a):
    hi = a.astype(BF16)
    lo = (a - hi.astype(F32)).astype(BF16)
    return hi, lo


def _dot(a, b):
    return jnp.dot(a, b, preferred_element_type=F32)


def _dot_nt(a, b):
    return lax.dot_general(a, b, (((1,), (1,)), ((), ())), preferred_element_type=F32)


def _silu(a):
    return a * jax.nn.sigmoid(a)


def _mod_kernel(c_ref, w_ref, b_ref, o_ref):
    s_hi, s_lo = _split_bf16(_silu(c_ref[...]))
    w_hi, w_lo = _split_bf16(w_ref[0])
    o_ref[0] = _dot(s_hi, w_hi) + _dot(s_hi, w_lo) + _dot(s_lo, w_hi) + b_ref[0]


def _modulation(cvec, ada_w, ada_b):
    depth, d, n = ada_w.shape
    return pl.pallas_call(
        _mod_kernel,
        grid=(depth, n // MOD_TILE_N),
        in_specs=[
            pl.BlockSpec((MOD_ROWS, d), lambda l, j: (0, 0)),
            pl.BlockSpec((1, d, MOD_TILE_N), lambda l, j: (l, 0, j)),
            pl.BlockSpec((1, 1, MOD_TILE_N), lambda l, j: (l, 0, j)),
        ],
        out_specs=pl.BlockSpec((1, MOD_ROWS, MOD_TILE_N), lambda l, j: (l, 0, j)),
        out_shape=jax.ShapeDtypeStruct((depth, MOD_ROWS, n), F32),
        compiler_params=_params("parallel", "parallel"),
        name="adaln_mod",
    )(cvec, ada_w, ada_b.reshape(depth, 1, n))


def _mod_spec(chunk, blocks_per_batch, ctx_row):
    def index(i):
        return (jnp.where(i % blocks_per_batch == 0, ctx_row, i // blocks_per_batch), 0, chunk)
    return pl.BlockSpec((1, 1, D_MODEL), index)


def _store_token_tiles(ref, rows):
    n = rows.shape[0]
    for s in range(TOKEN_SUBLANES):
        ref[pl.ds(s, n, stride=TOKEN_SUBLANES), :] = rows[:, s * LANES:(s + 1) * LANES]


def _load_token_tiles(ref, n, s):
    return ref[pl.ds(s, n, stride=TOKEN_SUBLANES), :]


def _stream_specs(stream):
    if not isinstance(stream, tuple):
        return [pl.BlockSpec((ROW_TILE, D_MODEL), lambda i: (i, 0))], [stream]
    assert CTX_LEN == ROW_TILE
    bpb = (CTX_LEN + SEQ) // ROW_TILE
    xpb = SEQ // ROW_TILE
    ctx_spec = pl.BlockSpec((ROW_TILE, D_MODEL), lambda i: (i // bpb, 0))
    x_spec = pl.BlockSpec((ROW_TILE, D_MODEL), lambda i: ((i // bpb) * xpb + jnp.maximum(i % bpb - 1, 0), 0))
    return [ctx_spec, x_spec], list(stream)


def _stream_rows(refs):
    if len(refs) == 1:
        return refs[0][...]
    is_ctx = pl.program_id(0) % ((CTX_LEN + SEQ) // ROW_TILE) == 0
    return jnp.where(is_ctx, refs[0][...], refs[1][...])


def _stream_len(stream):
    return sum(a.shape[0] for a in stream) if isinstance(stream, tuple) else stream.shape[0]


def _norm_modulate(x, g, shift, scale):
    ms = jnp.mean(x * x, axis=-1, keepdims=True)
    return (x * lax.rsqrt(ms + EPS) * g) * (1.0 + scale) + shift


def _qk_norm_rope(a, gmat_ref, perm_ref, cos_ref, sin_ref):
    w = a.shape[-1]
    sq_hi, sq_lo = _split_bf16(a * a)
    gmat = gmat_ref[:w, :w]
    msq = _dot(sq_hi, gmat) + _dot(sq_lo, gmat)
    rot = _dot(a.astype(BF16), perm_ref[:w, :w])
    return lax.rsqrt(msq + EPS) * (a * cos_ref[...] + rot * sin_ref[...])


def _in_even_kernel(n_x, *refs):
    x_refs = refs[:n_x]
    (g_ref, sh_ref, sc_ref, w_ref, cq_ref, sq_ref, ck_ref, sk_ref, gmat_ref, perm_ref,
     q_ref, k_ref, v_ref, u_ref) = refs[n_x:]
    x = _stream_rows(x_refs)
    hb = _norm_modulate(x, g_ref[...], sh_ref[0], sc_ref[0]).astype(BF16)
    c0, c1, c2 = A_WIDTH, A_WIDTH + A_KV_WIDTH, A_WIDTH + 2 * A_KV_WIDTH
    q = _dot(hb, w_ref[:, :c0])
    q_ref[...] = _qk_norm_rope(q, gmat_ref, perm_ref, cq_ref, sq_ref).astype(BF16)
    k = _qk_norm_rope(_dot(hb, w_ref[:, c0:c1]), gmat_ref, perm_ref, ck_ref, sk_ref)
    v = _dot(hb, w_ref[:, c1:c2])
    ones_col = (lax.broadcasted_iota(jnp.int32, (x.shape[0], LANES - A_HEAD_DIM), 1) == 0).astype(F32)
    for h in range(A_KV_HEADS):
        lanes = slice(h * A_HEAD_DIM, (h + 1) * A_HEAD_DIM)
        k_ref[h] = k[:, lanes].astype(BF16)
        v_ref[h] = jnp.concatenate([v[:, lanes], ones_col], axis=-1).astype(BF16)
    u_ref[...] = _dot(hb, w_ref[:, c2:])


def _rope_tables(q_gain, k_gain):
    t = np.arange(SEQ)
    pos = np.stack([t // GRID_W, t % GRID_W], axis=-1).astype(np.float32)
    inv = (ROPE_THETA ** (-np.arange(ROPE_FREQS, dtype=np.float32) / ROPE_FREQS)).astype(np.float32)
    ang = pos[:, :, None] * inv
    cos = np.concatenate([np.cos(ang), np.cos(ang)], axis=-1).reshape(SEQ, A_HEAD_DIM)
    sin = np.concatenate([-np.sin(ang), np.sin(ang)], axis=-1).reshape(SEQ, A_HEAD_DIM)
    cos = np.concatenate([np.ones((CTX_LEN, A_HEAD_DIM), np.float32), cos.astype(np.float32)], axis=0)
    sin = np.concatenate([np.zeros((CTX_LEN, A_HEAD_DIM), np.float32), sin.astype(np.float32)], axis=0)
    swap = _rope_swap()
    def tables(gain, heads, scale):
        c = jnp.asarray(cos) * (gain * scale)[None, :]
        s = jnp.asarray(sin) * (gain[swap] * scale)[None, :]
        return jnp.tile(c, (1, heads)), jnp.tile(s, (1, heads))
    cq, sq = tables(q_gain, A_HEADS, A_SCALE * LOG2_E)
    ck, sk = tables(k_gain, A_KV_HEADS, 1.0)
    return cq, sq, ck, sk


def _rope_swap():
    j = np.arange(A_HEAD_DIM)
    half = (j // ROPE_FREQS) % 2
    return np.where(half == 0, j + ROPE_FREQS, j - ROPE_FREQS)


def _head_matrices():
    lane = np.arange(A_WIDTH)
    same_head = (lane[:, None] // A_HEAD_DIM) == (lane[None, :] // A_HEAD_DIM)
    gmat = same_head.astype(np.float32) / A_HEAD_DIM
    swap = np.tile(_rope_swap(), A_HEADS) + (lane // A_HEAD_DIM) * A_HEAD_DIM
    perm = np.zeros((A_WIDTH, A_WIDTH), np.float32)
    perm[swap, lane] = 1.0
    return jnp.asarray(gmat, BF16), jnp.asarray(perm, BF16)


def _in_even(stream, mod_l, norm_g, w_in, q_gain, k_gain):
    t_rows = _stream_len(stream)
    x_specs, x_args = _stream_specs(stream)
    bpb = (CTX_LEN + SEQ) // ROW_TILE
    cq, sq, ck, sk = _rope_tables(q_gain, k_gain)
    gmat, perm = _head_matrices()
    n_in = w_in.shape[1]
    row = lambda w: pl.BlockSpec((ROW_TILE, w), lambda i: (i, 0))
    tab = lambda w: pl.BlockSpec((ROW_TILE, w), lambda i: (i % bpb, 0))
    full = lambda a, b: pl.BlockSpec((a, b), lambda i: (0, 0))
    return pl.pallas_call(
        functools.partial(_in_even_kernel, len(x_specs)),
        grid=(t_rows // ROW_TILE,),
        in_specs=x_specs + [
            full(1, D_MODEL), _mod_spec(0, bpb, BATCH), _mod_spec(1, bpb, BATCH),
            full(D_MODEL, n_in), tab(A_WIDTH), tab(A_WIDTH), tab(A_KV_WIDTH), tab(A_KV_WIDTH),
            full(A_WIDTH, A_WIDTH), full(A_WIDTH, A_WIDTH),
        ],
        out_specs=[row(A_WIDTH),
                   pl.BlockSpec((A_KV_HEADS, ROW_TILE, A_HEAD_DIM), lambda i: (0, i, 0)),
                   pl.BlockSpec((A_KV_HEADS, ROW_TILE, LANES), lambda i: (0, i, 0)),
                   row(B_WIDTH)],
        out_shape=[
            jax.ShapeDtypeStruct((t_rows, A_WIDTH), BF16),
            jax.ShapeDtypeStruct((A_KV_HEADS, t_rows, A_HEAD_DIM), BF16),
            jax.ShapeDtypeStruct((A_KV_HEADS, t_rows, LANES), BF16),
            jax.ShapeDtypeStruct((t_rows, B_WIDTH), F32),
        ],
        compiler_params=_params("parallel"),
        name="even_in_proj",
    )(*x_args, norm_g.reshape(1, D_MODEL), mod_l, mod_l, w_in.astype(BF16), cq, sq, ck, sk, gmat, perm)


def _gqa_kernel(q_ref, k_ref, v_ref, o_ref):
    def attend(n_keys):
        k = k_ref[0, :n_keys, :]
        v = v_ref[0, :n_keys, :]
        outs = []
        for h in range(A_HEADS // A_KV_HEADS):
            s = _dot_nt(q_ref[:, h * A_HEAD_DIM:(h + 1) * A_HEAD_DIM], k)
            m = jnp.max(s, axis=-1, keepdims=True)
            p = jnp.exp2(s - m).astype(BF16)
            acc = _dot(p, v)
            outs.append(acc[:, :A_HEAD_DIM] / acc[:, A_HEAD_DIM:A_HEAD_DIM + 1])
        o_ref[...] = jnp.concatenate(outs, axis=-1).astype(BF16)

    is_ctx = pl.program_id(1) == 0

    @pl.when(is_ctx)
    def _():
        attend(CTX_LEN)

    @pl.when(jnp.logical_not(is_ctx))
    def _():
        attend(CTX_LEN + SEQ)


def _gqa(q, k_heads, v_heads):
    t_rows = q.shape[0]
    r = CTX_LEN + SEQ
    bpb = r // ROW_TILE
    group_w = (A_HEADS // A_KV_HEADS) * A_HEAD_DIM
    return pl.pallas_call(
        _gqa_kernel,
        grid=(BATCH, bpb, A_KV_HEADS),
        in_specs=[
            pl.BlockSpec((ROW_TILE, group_w), lambda b, j, g: (b * bpb + j, g)),
            pl.BlockSpec((1, r, A_HEAD_DIM), lambda b, j, g: (g, b, 0)),
            pl.BlockSpec((1, r, LANES), lambda b, j, g: (g, b, 0)),
        ],
        out_specs=pl.BlockSpec((ROW_TILE, group_w), lambda b, j, g: (b * bpb + j, g)),
        out_shape=jax.ShapeDtypeStruct((t_rows, A_WIDTH), BF16),
        compiler_params=_params("parallel", "parallel", "parallel"),
        name="gqa_attention",
    )(q, k_heads, v_heads)


def _pool_kernel(u_ref, w_ref, s_ref, o_ref, pad_ref):
    g = pl.program_id(1)
    w_bf = w_ref[0].astype(BF16)
    for start, length in ((0, CTX_LEN), (CTX_LEN, SEQ)):
        u = u_ref[start:start + length, :]
        pad_ref[0:POOL_PAD, :] = jnp.zeros((POOL_PAD, LANES), F32)
        pad_ref[POOL_PAD:POOL_PAD + length, :] = u
        pad_ref[POOL_PAD + length:2 * POOL_PAD + length, :] = jnp.zeros((POOL_PAD, LANES), F32)
        t = lax.broadcasted_iota(jnp.int32, (length, LANES), 0)
        for gi, win in enumerate(POOL_WINDOWS):
            @pl.when(g == gi)
            def _(win=win, u=u, t=t, start=start, length=length):
                base = POOL_PAD - win // 2
                acc = pad_ref[base:base + length, :]
                for j in range(1, win):
                    acc = acc + pad_ref[base + j:base + j + length, :]
                lo = jnp.clip(t - win // 2, 0, length)
                hi = jnp.clip(t - win // 2 + win, 0, length)
                pooled = acc / (hi - lo).astype(F32) - u
                mixed = _dot(pooled.astype(BF16), w_bf) * s_ref[0]
                o_ref[start:start + length, :] = mixed.astype(BF16)


def _pool(u, pool_w, pool_scale):
    t_rows = u.shape[0]
    r = CTX_LEN + SEQ
    return pl.pallas_call(
        _pool_kernel,
        grid=(BATCH, B_GROUPS),
        in_specs=[
            pl.BlockSpec((r, B_GROUP_DIM), lambda b, g: (b, g)),
            pl.BlockSpec((1, B_GROUP_DIM, B_GROUP_DIM), lambda b, g: (g, 0, 0)),
            pl.BlockSpec((1, 1, B_GROUP_DIM), lambda b, g: (g, 0, 0)),
        ],
        out_specs=pl.BlockSpec((r, B_GROUP_DIM), lambda b, g: (b, g)),
        out_shape=jax.ShapeDtypeStruct((t_rows, B_WIDTH), BF16),
        scratch_shapes=[pltpu.VMEM((SEQ + 2 * POOL_PAD, B_GROUP_DIM), F32)],
        compiler_params=_params("parallel", "parallel"),
        name="pool_mixer",
    )(u, pool_w, pool_scale.reshape(B_GROUPS, 1, B_GROUP_DIM))


def _in_odd_kernel(x_ref, g_ref, sh_ref, sc_ref, w_ref, q_ref, k_ref, v_ref):
    hb = _norm_modulate(x_ref[...], g_ref[...], sh_ref[0], sc_ref[0]).astype(BF16)
    q_ref[...] = (_dot(hb, w_ref[:, :C_WIDTH]) * (C_SCALE * LOG2_E)).astype(BF16)
    k_ref[...] = _dot(hb, w_ref[:, C_WIDTH:2 * C_WIDTH]).astype(BF16)
    v_ref[...] = _dot(hb, w_ref[:, 2 * C_WIDTH:]).astype(BF16)


def _in_odd(xs, mod_l, norm_g, w_in):
    t_rows = xs.shape[0]
    bpb = (CTX_LEN + SEQ) // ROW_TILE
    row = lambda w: pl.BlockSpec((ROW_TILE, w), lambda i: (i, 0))
    full = lambda a, b: pl.BlockSpec((a, b), lambda i: (0, 0))
    return pl.pallas_call(
        _in_odd_kernel,
        grid=(t_rows // ROW_TILE,),
        in_specs=[row(D_MODEL), full(1, D_MODEL), _mod_spec(0, bpb, BATCH), _mod_spec(1, bpb, BATCH),
                  full(D_MODEL, 3 * C_WIDTH)],
        out_specs=[row(C_WIDTH)] * 3,
        out_shape=[jax.ShapeDtypeStruct((t_rows, C_WIDTH), BF16)] * 3,
        compiler_params=_params("parallel"),
        name="odd_in_proj",
    )(xs, norm_g.reshape(1, D_MODEL), mod_l, mod_l, w_in.astype(BF16))


def _na_plan():
    rows = SEQ // GRID_W
    kh = min(NA_ROWS_MAX, rows)
    win_rows = min(kh + NA_BAND - 1, rows)
    starts, types, patterns = [], [], []
    for r0 in range(0, rows, NA_BAND):
        ws = int(np.clip(r0 - kh // 2, 0, rows - win_rows))
        pat = np.full((NA_BAND, win_rows), -1, np.int64)
        for rr in range(NA_BAND):
            r = r0 + rr
            rs = int(np.clip(r - kh // 2, 0, rows - kh))
            for wi in range(win_rows):
                if rs <= ws + wi < rs + kh:
                    pat[rr, wi] = ws + wi - r + (NA_ROWS_MAX - 1)
        for t, known in enumerate(patterns):
            if np.array_equal(known, pat):
                types.append(t)
                break
        else:
            types.append(len(patterns))
            patterns.append(pat)
        starts.append(ws)
    return win_rows, np.asarray(starts, np.int32), np.asarray(types, np.int32), patterns


def _na_strip_plan(patterns):
    n_off = 2 * NA_ROWS_MAX - 1
    cuts = []
    for pat in patterns:
        rows = []
        for rr in range(pat.shape[0]):
            valid = np.nonzero(pat[rr] >= 0)[0]
            lo, hi = int(valid[0]), int(valid[-1]) + 1
            delta = int(pat[rr, lo]) - lo
            assert np.array_equal(pat[rr, lo:hi], np.arange(lo, hi) + delta)
            rows.append((delta, lo, hi))
        cuts.append(rows)
    win_rows = patterns[0].shape[1]
    deltas = [d for rows in cuts for d, _, _ in rows]
    pad_lo = max(0, -min(deltas))
    pad_hi = max(0, max(deltas) + win_rows - n_off)
    pad_hi += (pad_lo + n_off + pad_hi) % 2
    plan = tuple(tuple((d + pad_lo, lo, hi) for d, lo, hi in rows) for rows in cuts)
    return plan, pad_lo, pad_hi


def _na_bias_strip(rel_bias, pad_lo, pad_hi):
    cq = np.arange(GRID_W)
    col_start = np.clip(cq - NA_COLS // 2, 0, GRID_W - NA_COLS)
    ck = np.arange(GRID_W)
    inside = (ck[None, :] >= col_start[:, None]) & (ck[None, :] < col_start[:, None] + NA_COLS)
    shift = GRID_W - NA_COLS
    padded = jnp.pad(rel_bias.astype(F32) * LOG2_E, ((0, 0), (0, 0), (shift, shift)))
    tcol = jnp.stack([padded[:, :, GRID_W - 1 - c:2 * GRID_W - 1 - c] for c in range(GRID_W)], axis=1)
    tcol = jnp.where(jnp.asarray(inside)[None, :, None, :], tcol, MASKED)
    strip = tcol.reshape(C_HEADS, GRID_W, -1)
    return jnp.pad(strip, ((0, 0), (0, 0), (pad_lo * GRID_W, pad_hi * GRID_W)), constant_values=MASKED)


def _na_kernel(plan, ws_ref, ty_ref, q_ref, k_ref, v_ref, strip_ref, o_ref, bias_ref):
    n_bands = (SEQ // GRID_W) // NA_BAND
    band_q = NA_BAND * GRID_W
    win = bias_ref.shape[-1]

    @pl.when(pl.program_id(1) == 0)
    def _():
        win_row = lax.broadcasted_iota(jnp.int32, (GRID_W, win), 1) // GRID_W
        for h in range(2):
            for t, rows in enumerate(plan):
                for rr, (block, lo, hi) in enumerate(rows):
                    cut = strip_ref[h, :, block * GRID_W:block * GRID_W + win]
                    own = jnp.logical_and(win_row >= lo, win_row < hi)
                    bias_ref[h, t, rr * GRID_W:(rr + 1) * GRID_W, :] = jnp.where(own, cut, MASKED)

    o_ref[0:CTX_LEN, :] = jnp.zeros((CTX_LEN, LANES), BF16)
    kc = k_ref[0:CTX_LEN, :]
    vc = v_ref[0:CTX_LEN, :]
    first = lax.broadcasted_iota(jnp.int32, (1, LANES), 1) < C_HEAD_DIM

    def one_band(b, carry):
        q_off = pl.multiple_of(CTX_LEN + b * band_q, GRID_W)
        k_off = pl.multiple_of(CTX_LEN + ws_ref[b] * GRID_W, GRID_W)
        qb = q_ref[pl.ds(q_off, band_q), :]
        kw = k_ref[pl.ds(k_off, win), :]
        vw = v_ref[pl.ds(k_off, win), :]
        outs = []
        for h in range(2):
            qm = jnp.where(first if h == 0 else jnp.logical_not(first), qb, jnp.zeros_like(qb))
            s_loc = _dot_nt(qm, kw) + bias_ref[h, ty_ref[b]]
            s_ctx = _dot_nt(qm, kc)
            m = jnp.maximum(jnp.max(s_loc, axis=-1, keepdims=True), jnp.max(s_ctx, axis=-1, keepdims=True))
            p_loc = jnp.exp2(s_loc - m)
            p_ctx = jnp.exp2(s_ctx - m)
            denom = jnp.sum(p_loc, axis=-1, keepdims=True) + jnp.sum(p_ctx, axis=-1, keepdims=True)
            o = _dot(p_loc.astype(BF16), vw) + _dot(p_ctx.astype(BF16), vc)
            outs.append(o / denom)
        o_ref[pl.ds(q_off, band_q), :] = jnp.where(first, outs[0], outs[1]).astype(BF16)
        return carry

    lax.fori_loop(0, n_bands, one_band, 0, unroll=2)


def _na(q, k, v, rel_bias):
    t_rows = q.shape[0]
    r = CTX_LEN + SEQ
    win_rows, starts, types, patterns = _na_plan()
    plan, pad_lo, pad_hi = _na_strip_plan(patterns)
    strip = _na_bias_strip(rel_bias, pad_lo, pad_hi)
    blk = pl.BlockSpec((r, LANES), lambda p, b, ws, ty: (b, p))
    return pl.pallas_call(
        functools.partial(_na_kernel, plan),
        grid_spec=pltpu.PrefetchScalarGridSpec(
            num_scalar_prefetch=2,
            grid=(C_HEADS // 2, BATCH),
            in_specs=[blk, blk, blk,
                      pl.BlockSpec((2, GRID_W, strip.shape[-1]), lambda p, b, ws, ty: (p, 0, 0))],
            out_specs=blk,
            scratch_shapes=[pltpu.VMEM((2, len(patterns), NA_BAND * GRID_W, win_rows * GRID_W), F32)],
        ),
        out_shape=jax.ShapeDtypeStruct((t_rows, C_WIDTH), BF16),
        compiler_params=_params("parallel", "arbitrary"),
        name="neighbourhood_attention",
    )(jnp.asarray(starts), jnp.asarray(types), q, k, v, strip)


def _out_kernel(n_lat, n_x, *refs):
    lat_refs = refs[:n_lat]
    w_ref = refs[n_lat]
    x_refs = refs[n_lat + 1:n_lat + 1 + n_x]
    g1_ref, g_ref, sh_ref, sc_ref, wr_ref, br_ref, xo_ref, h_ref, lg_ref = refs[n_lat + 1 + n_x:]
    y = None
    off = 0
    for lat in lat_refs:
        width = lat.shape[-1]
        part = _dot(lat[...], w_ref[off:off + width, :])
        y = part if y is None else y + part
        off += width
    x_new = _stream_rows(x_refs) + g1_ref[0] * y
    xo_ref[...] = x_new
    h = _norm_modulate(x_new, g_ref[...], sh_ref[0], sc_ref[0])
    _store_token_tiles(h_ref, h)
    h_hi, h_lo = _split_bf16(h)
    both = _dot(h_hi, wr_ref[...])
    lg_ref[...] = (both[:, :ROUTE_LANES] + both[:, ROUTE_LANES:]
                   + _dot(h_lo, wr_ref[:, :ROUTE_LANES]) + br_ref[...])


def _router_weights(wg, bg, we, be):
    gap = ROUTE_EXPERT_ROW - N_GROUPS
    tail = ROUTE_LANES - ROUTE_EXPERT_ROW - N_EXPERTS
    d = wg.shape[0]
    w = jnp.concatenate([wg, jnp.zeros((d, gap), F32), we, jnp.zeros((d, tail), F32)], axis=1)
    w_hi, w_lo = _split_bf16(w)
    b = jnp.concatenate([bg, jnp.zeros((gap,), F32), be, jnp.zeros((tail,), F32)])
    return jnp.concatenate([w_hi, w_lo], axis=1), b.reshape(1, ROUTE_LANES)


def _out_proj(lats, w_out, stream, mod_l, norm_g, wr, br):
    t_rows = _stream_len(stream)
    x_specs, x_args = _stream_specs(stream)
    bpb = (CTX_LEN + SEQ) // ROW_TILE
    row = lambda w: pl.BlockSpec((ROW_TILE, w), lambda i: (i, 0))
    full = lambda a, b: pl.BlockSpec((a, b), lambda i: (0, 0))
    return pl.pallas_call(
        functools.partial(_out_kernel, len(lats), len(x_specs)),
        grid=(t_rows // ROW_TILE,),
        in_specs=[row(l.shape[1]) for l in lats] + [full(*w_out.shape)] + x_specs + [
            _mod_spec(2, bpb, BATCH), full(1, D_MODEL),
            _mod_spec(3, bpb, BATCH), _mod_spec(4, bpb, BATCH), full(D_MODEL, 2 * ROUTE_LANES), full(1, ROUTE_LANES),
        ],
        out_specs=[row(D_MODEL), pl.BlockSpec((ROW_TILE * TOKEN_SUBLANES, LANES), lambda i: (i, 0)),
                   row(ROUTE_LANES)],
        out_shape=[
            jax.ShapeDtypeStruct((t_rows, D_MODEL), F32),
            jax.ShapeDtypeStruct((t_rows * TOKEN_SUBLANES, LANES), F32),
            jax.ShapeDtypeStruct((t_rows, ROUTE_LANES), F32),
        ],
        compiler_params=_params("parallel"),
        name="out_proj_ffn_norm",
    )(*lats, w_out.astype(BF16), *x_args, mod_l, norm_g.reshape(1, D_MODEL), mod_l, mod_l, wr, br)


def _route_kernel(lg_ref, tri_ref, e_ref, rk_ref, wt_ref, cnt_ref, run_ref):
    @pl.when(pl.program_id(0) == 0)
    def _():
        run_ref[...] = jnp.zeros(run_ref.shape, F32)

    lt = lg_ref[...].T
    sub = lax.broadcasted_iota(jnp.int32, (E_PER_GROUP, ROW_TILE), 0)
    gl = jnp.where(sub < N_GROUPS, lt[0:E_PER_GROUP], -jnp.inf)
    g_max = jnp.max(gl, axis=0, keepdims=True)
    g_idx = jnp.min(jnp.where(gl == g_max, sub, E_PER_GROUP), axis=0, keepdims=True)
    g_w = 1.0 / jnp.sum(jnp.exp(gl - g_max), axis=0, keepdims=True)
    sel = lt[ROUTE_EXPERT_ROW:ROUTE_EXPERT_ROW + E_PER_GROUP]
    for g in range(1, N_GROUPS):
        lo = ROUTE_EXPERT_ROW + g * E_PER_GROUP
        sel = jnp.where(g_idx == g, lt[lo:lo + E_PER_GROUP], sel)
    m1 = jnp.max(sel, axis=0, keepdims=True)
    i1 = jnp.min(jnp.where(sel == m1, sub, E_PER_GROUP), axis=0, keepdims=True)
    rest = jnp.where(sub == i1, -jnp.inf, sel)
    m2 = jnp.max(rest, axis=0, keepdims=True)
    i2 = jnp.min(jnp.where(rest == m2, sub, E_PER_GROUP), axis=0, keepdims=True)
    t = jnp.exp(m2 - m1)
    p1 = 1.0 / (1.0 + t)
    wt_ref[0] = jnp.concatenate([g_w * p1, g_w * (t * p1)], axis=0)
    e_flat = jnp.concatenate([g_idx * E_PER_GROUP + i1, g_idx * E_PER_GROUP + i2], axis=1)
    e_ref[0] = e_flat
    hit = lax.broadcasted_iota(jnp.int32, (N_EXPERTS, TOP_K * ROW_TILE), 0) == e_flat
    onehot = jnp.where(hit, 1.0, 0.0)
    before = _dot(onehot.astype(BF16), tri_ref[...]) + run_ref[:, 0:1]
    rk_ref[0] = jnp.sum(jnp.where(hit, before, 0.0), axis=0, keepdims=True).astype(jnp.int32)
    run_ref[...] = run_ref[...] + jnp.sum(onehot, axis=1, keepdims=True)
    cnt_ref[...] = run_ref[...]


def _route(logits):
    t_rows = logits.shape[0]
    n_blk = t_rows // ROW_TILE
    n_asg = TOP_K * ROW_TILE
    tri = jnp.asarray(np.triu(np.ones((n_asg, n_asg), np.float32), 1), BF16)
    e_blk, rank_blk, wt_blk, counts = pl.pallas_call(
        _route_kernel,
        grid=(n_blk,),
        in_specs=[pl.BlockSpec((ROW_TILE, ROUTE_LANES), lambda i: (i, 0)),
                  pl.BlockSpec((n_asg, n_asg), lambda i: (0, 0))],
        out_specs=[pl.BlockSpec((1, 1, n_asg), lambda i: (i, 0, 0)),
                   pl.BlockSpec((1, 1, n_asg), lambda i: (i, 0, 0)),
                   pl.BlockSpec((1, TOP_K, ROW_TILE), lambda i: (i, 0, 0)),
                   pl.BlockSpec((N_EXPERTS, LANES), lambda i: (0, 0))],
        out_shape=[jax.ShapeDtypeStruct((n_blk, 1, n_asg), jnp.int32),
                   jax.ShapeDtypeStruct((n_blk, 1, n_asg), jnp.int32),
                   jax.ShapeDtypeStruct((n_blk, TOP_K, ROW_TILE), F32),
                   jax.ShapeDtypeStruct((N_EXPERTS, LANES), F32)],
        scratch_shapes=[pltpu.VMEM((N_EXPERTS, LANES), F32)],
        compiler_params=_params("arbitrary"),
        name="moe_route",
    )(logits, tri)
    counts = counts[:, 0].astype(jnp.int32)
    blocks_of = (counts + MOE_ROWS - 1) // MOE_ROWS
    b_end = jnp.cumsum(blocks_of)
    slot_base = ((b_end - blocks_of) * MOE_ROWS).astype(jnp.int32)
    n_used = b_end[-1].astype(jnp.int32)
    blk = jnp.minimum(jnp.arange(_moe_blocks(t_rows), dtype=jnp.int32), n_used - 1)
    block_e = jnp.minimum(jnp.sum((b_end[None, :] <= blk[:, None]).astype(jnp.int32), axis=1), N_EXPERTS - 1)
    weights = wt_blk.transpose(0, 2, 1).reshape(t_rows, TOP_K)
    return (slot_base, counts, e_blk, rank_blk), weights, block_e, n_used.reshape(1)


def _moe_blocks(t_rows):
    return (t_rows * TOP_K) // MOE_ROWS + N_EXPERTS


def _tile_copy(src_ref, src_token, dst_ref, dst_token, sem):
    src = src_ref.at[pl.ds(pl.multiple_of(src_token * TOKEN_SUBLANES, TOKEN_SUBLANES), TOKEN_SUBLANES)]
    dst = dst_ref.at[pl.ds(pl.multiple_of(dst_token * TOKEN_SUBLANES, TOKEN_SUBLANES), TOKEN_SUBLANES)]
    return pltpu.make_async_copy(src, dst, sem)


def _start_rows(copies):
    def start(r, c):
        for n, cp in enumerate(copies(r)):
            cp.start(priority=n % 2)
        return c

    lax.fori_loop(0, ROW_TILE, start, 0, unroll=DMA_UNROLL)


def _wait_rows(copies):
    def wait(r, c):
        for cp in copies(r):
            cp.wait()
        return c

    lax.fori_loop(0, ROW_TILE, wait, 0, unroll=DMA_UNROLL)


def _start_then_wait(copies):
    _start_rows(copies)
    _wait_rows(copies)


def _slot(base_ref, e_ref, rk_ref, j):
    return base_ref[e_ref[0, 0, j]] + rk_ref[0, 0, j]


def _dispatch_kernel(base_ref, cnt_ref, e_ref, rk_ref, h_ref, xpad_ref, zero_ref, sem):
    _start_then_wait(lambda r: [
        _tile_copy(h_ref, r, xpad_ref, _slot(base_ref, e_ref, rk_ref, k * ROW_TILE + r), sem)
        for k in range(TOP_K)])

    @pl.when(pl.program_id(0) == pl.num_programs(0) - 1)
    def _():
        zero_ref[...] = jnp.zeros(zero_ref.shape, F32)
        block_end = lambda e: base_ref[e] + (cnt_ref[e] + MOE_ROWS - 1) // MOE_ROWS * MOE_ROWS

        def pad_expert(e, c):
            lo = base_ref[e] + cnt_ref[e]
            zero_copy = lambda j: _tile_copy(zero_ref, 0, xpad_ref, j, sem)
            lax.fori_loop(lo, block_end(e), lambda j, cc: (zero_copy(j).start(), cc)[1], 0)
            lax.fori_loop(lo, block_end(e), lambda j, cc: (zero_copy(j).wait(), cc)[1], 0)
            return c

        lax.fori_loop(0, N_EXPERTS, pad_expert, 0)
        block_rows = MOE_ROWS * TOKEN_SUBLANES
        first_unused = block_end(N_EXPERTS - 1) // MOE_ROWS
        n_blocks = xpad_ref.shape[0] // block_rows
        block_copy = lambda b: pltpu.make_async_copy(
            zero_ref, xpad_ref.at[pl.ds(pl.multiple_of(b * block_rows, block_rows), block_rows)], sem)
        lax.fori_loop(first_unused, n_blocks, lambda b, cc: (block_copy(b).start(), cc)[1], 0)
        lax.fori_loop(first_unused, n_blocks, lambda b, cc: (block_copy(b).wait(), cc)[1], 0)


def _dispatch(h_tiles, slots):
    slot_base, counts, e_blk, rank_blk = slots
    t_rows = h_tiles.shape[0] // TOKEN_SUBLANES
    n_pad = _moe_blocks(t_rows) * MOE_ROWS
    asg = pl.BlockSpec((1, 1, TOP_K * ROW_TILE), lambda i, base, cnt: (i, 0, 0), memory_space=pltpu.SMEM)
    return pl.pallas_call(
        _dispatch_kernel,
        grid_spec=pltpu.PrefetchScalarGridSpec(
            num_scalar_prefetch=2,
            grid=(t_rows // ROW_TILE,),
            in_specs=[asg, asg,
                      pl.BlockSpec((ROW_TILE * TOKEN_SUBLANES, LANES), lambda i, base, cnt: (i, 0))],
            out_specs=pl.BlockSpec(memory_space=pl.ANY),
            scratch_shapes=[pltpu.VMEM((MOE_ROWS * TOKEN_SUBLANES, LANES), F32), pltpu.SemaphoreType.DMA(())],
        ),
        out_shape=jax.ShapeDtypeStruct((n_pad * TOKEN_SUBLANES, LANES), F32),
        compiler_params=_params("arbitrary"),
        name="moe_dispatch",
    )(slot_base, counts, e_blk, rank_blk, h_tiles)


def _expert_kernel(be_ref, nu_ref, x_ref, w1_ref, w3_ref, w2_ref, y_ref, w1_bf, w3_bf, w2_bf):
    i = pl.program_id(0)
    fresh = jnp.logical_or(i == 0, be_ref[i] != be_ref[jnp.maximum(i - 1, 0)])

    @pl.when(fresh)
    def _():
        w1_bf[...] = w1_ref[0, 0].astype(BF16)
        w3_bf[...] = w3_ref[0, 0].astype(BF16)
        w2_bf[...] = w2_ref[0, 0].astype(BF16)

    used = i < nu_ref[0]

    @pl.when(used)
    def _():
        xb = jnp.concatenate(
            [_load_token_tiles(x_ref, MOE_ROWS, s).astype(BF16) for s in range(TOKEN_SUBLANES)], axis=-1)
        hid = _silu(_dot(xb, w1_bf[...])) * _dot(xb, w3_bf[...])
        _store_token_tiles(y_ref, _dot(hid.astype(BF16), w2_bf[...]))

    @pl.when(jnp.logical_not(used))
    def _():
        y_ref[...] = jnp.zeros(y_ref.shape, F32)


def _experts(xpad, block_e, n_used, w1, w3, w2, layer):
    g_blocks = xpad.shape[0] // (MOE_ROWS * TOKEN_SUBLANES)
    tile_rows = MOE_ROWS * TOKEN_SUBLANES
    x_index = lambda i, be, nu: (jnp.minimum(i, nu[0] - 1), 0)
    w_index = lambda i, be, nu: (layer, be[i], 0, 0)
    return pl.pallas_call(
        _expert_kernel,
        grid_spec=pltpu.PrefetchScalarGridSpec(
            num_scalar_prefetch=2,
            grid=(g_blocks,),
            in_specs=[
                pl.BlockSpec((tile_rows, LANES), x_index),
                pl.BlockSpec((1, 1, D_MODEL, D_EXPERT), w_index),
                pl.BlockSpec((1, 1, D_MODEL, D_EXPERT), w_index),
                pl.BlockSpec((1, 1, D_EXPERT, D_MODEL), w_index),
            ],
            out_specs=pl.BlockSpec((tile_rows, LANES), lambda i, be, nu: (i, 0)),
            scratch_shapes=[
                pltpu.VMEM((D_MODEL, D_EXPERT), BF16),
                pltpu.VMEM((D_MODEL, D_EXPERT), BF16),
                pltpu.VMEM((D_EXPERT, D_MODEL), BF16),
            ],
        ),
        out_shape=jax.ShapeDtypeStruct((g_blocks * tile_rows, LANES), F32),
        compiler_params=_params("arbitrary"),
        name="moe_experts",
    )(block_e, n_used, xpad, w1, w3, w2)


def _combine_kernel(final, base_ref, e_ref, rk_ref, e_next, rk_next, x_ref, wt_ref, g2_ref, fg_ref, ypad_ref,
                    o_ref, ybuf, sems):
    i = pl.program_id(0)
    slot = i % 2

    def gathers(e, rk, into):
        return lambda r: [
            _tile_copy(ypad_ref, _slot(base_ref, e, rk, k * ROW_TILE + r), ybuf.at[into, k], r, sems.at[into])
            for k in range(TOP_K)]

    @pl.when(i == 0)
    def _():
        _start_rows(gathers(e_ref, rk_ref, slot))

    @pl.when(i + 1 < pl.num_programs(0))
    def _():
        _start_rows(gathers(e_next, rk_next, 1 - slot))

    _wait_rows(gathers(e_ref, rk_ref, slot))
    wt = wt_ref[...]
    parts = []
    for s in range(TOKEN_SUBLANES):
        lanes = slice(s * LANES, (s + 1) * LANES)
        f = (wt[:, 0:1] * _load_token_tiles(ybuf.at[slot, 0], ROW_TILE, s)
             + wt[:, 1:2] * _load_token_tiles(ybuf.at[slot, 1], ROW_TILE, s))
        parts.append(x_ref[:, lanes] + g2_ref[0][:, lanes] * f)
    x_new = jnp.concatenate(parts, axis=-1)
    if final:
        ms = jnp.mean(x_new * x_new, axis=-1, keepdims=True)
        x_new = x_new * lax.rsqrt(ms + EPS) * fg_ref[...]
    o_ref[...] = x_new


def _combine(ypad, slots, weights, xs, mod_l, final_g, final):
    slot_base, _, e_blk, rank_blk = slots
    t_rows = xs.shape[0]
    bpb = (CTX_LEN + SEQ) // ROW_TILE
    if final:
        xpb = SEQ // ROW_TILE
        n_out = BATCH * xpb
        blk = lambda i: (i // xpb) * bpb + i % xpb + 1
        g2_index = lambda i, base: (i // xpb, 0, 5)
    else:
        n_out = t_rows // ROW_TILE
        blk = lambda i: i
        g2_index = lambda i, base: (jnp.where(i % bpb == 0, BATCH, i // bpb), 0, 5)
    asg = pl.BlockSpec((1, 1, TOP_K * ROW_TILE), lambda i, base: (blk(i), 0, 0), memory_space=pltpu.SMEM)
    asg_next = pl.BlockSpec((1, 1, TOP_K * ROW_TILE), lambda i, base: (blk(jnp.minimum(i + 1, n_out - 1)), 0, 0),
                            memory_space=pltpu.SMEM)
    return pl.pallas_call(
        functools.partial(_combine_kernel, final),
        grid_spec=pltpu.PrefetchScalarGridSpec(
            num_scalar_prefetch=1,
            grid=(n_out,),
            in_specs=[
                asg, asg, asg_next, asg_next,
                pl.BlockSpec((ROW_TILE, D_MODEL), lambda i, base: (blk(i), 0)),
                pl.BlockSpec((ROW_TILE, TOP_K), lambda i, base: (blk(i), 0)),
                pl.BlockSpec((1, 1, D_MODEL), g2_index),
                pl.BlockSpec((1, D_MODEL), lambda i, base: (0, 0)),
                pl.BlockSpec(memory_space=pl.ANY),
            ],
            out_specs=pl.BlockSpec((ROW_TILE, D_MODEL), lambda i, base: (i, 0)),
            scratch_shapes=[pltpu.VMEM((2, TOP_K, ROW_TILE * TOKEN_SUBLANES, LANES), F32),
                            pltpu.SemaphoreType.DMA((2,))],
        ),
        out_shape=jax.ShapeDtypeStruct((n_out * ROW_TILE, D_MODEL), F32),
        compiler_params=_params("arbitrary"),
        name="moe_combine_final" if final else "moe_combine",
    )(slot_base, e_blk, rank_blk, e_blk, rank_blk, xs, weights, mod_l, final_g.reshape(1, D_MODEL), ypad)


def _moe(h_tiles, logits, xs, mod_l, w1, w3, w2, layer, final_g, final):
    slots, weights, block_e, n_used = _route(logits)
    xpad = _dispatch(h_tiles, slots)
    ypad = _experts(xpad, block_e, n_used, w1, w3, w2, layer)
    return _combine(ypad, slots, weights, xs, mod_l, final_g, final)


def kernel(x, c, ctx, c_ctx, ada_w, ada_b, norm_mix_g, norm_ffn_g, even_w_in, even_w_out, a_q_gain, a_k_gain, pool_w, pool_scale, odd_w_in, odd_w_out, na_rel_bias, moe_w_group, moe_b_group, moe_w_expert, moe_b_expert, moe_w1, moe_w3, moe_w2, final_g):
    xs = (ctx.reshape(BATCH * CTX_LEN, D_MODEL), x.reshape(BATCH * SEQ, D_MODEL))
    cvec = jnp.concatenate([c, c_ctx[None, :], jnp.zeros((MOD_ROWS - BATCH - 1, D_MODEL), F32)], axis=0)
    mod = _modulation(cvec, ada_w, ada_b)
    for l in range(DEPTH):
        mod_l = mod[l].reshape(MOD_ROWS, 1, N_MOD * D_MODEL)
        i = l // 2
        if l % 2 == 0:
            q, k, v, u = _in_even(xs, mod_l, norm_mix_g[l], even_w_in[i], a_q_gain[i], a_k_gain[i])
            lats = [_gqa(q, k, v), _pool(u, pool_w[i], pool_scale[i])]
            w_out = even_w_out[i]
        else:
            q, k, v = _in_odd(xs, mod_l, norm_mix_g[l], odd_w_in[i])
            lats = [_na(q, k, v, na_rel_bias[i])]
            w_out = odd_w_out[i]
        wr, br = _router_weights(moe_w_group[l], moe_b_group[l], moe_w_expert[l], moe_b_expert[l])
        xs, h, logits = _out_proj(lats, w_out, xs, mod_l, norm_ffn_g[l], wr, br)
        xs = _moe(h, logits, xs, mod_l, moe_w1, moe_w3, moe_w2, l, final_g, l == DEPTH - 1)
    return xs.reshape(BATCH, SEQ, D_MODEL)
```

```python
import functools

import jax
import jax.numpy as jnp
import numpy as np
from jax import lax
from jax.experimental import pallas as pl
from jax.experimental.pallas import tpu as pltpu

D_MODEL = 1024
BATCH = 8
SEQ = 4096
DEPTH = 2
GRID_W = 64
CTX_LEN = 256
EPS = 1e-6
N_MOD = 6

A_HEADS = 8
A_KV_HEADS = 2
A_HEAD_DIM = 64
A_WIDTH = A_HEADS * A_HEAD_DIM
A_KV_WIDTH = A_KV_HEADS * A_HEAD_DIM
A_SCALE = A_HEAD_DIM ** -0.5
ROPE_THETA = 10000.0
ROPE_FREQS = A_HEAD_DIM // 4

B_GROUPS = 4
B_WIDTH = D_MODEL // 2
B_GROUP_DIM = B_WIDTH // B_GROUPS
POOL_WINDOWS = (2, 4, 8, 16)
POOL_PAD = 16

C_HEADS = 16
C_HEAD_DIM = D_MODEL // C_HEADS
C_WIDTH = C_HEADS * C_HEAD_DIM
C_SCALE = C_HEAD_DIM ** -0.5
NA_ROWS_MAX = 8
NA_COLS = 16
NA_BAND = 4
LOG2_E = 1.4426950408889634

N_GROUPS = 4
E_PER_GROUP = 8
N_EXPERTS = N_GROUPS * E_PER_GROUP
TOP_K = 2
D_EXPERT = D_MODEL // 2

LANES = 128
TOKEN_SUBLANES = D_MODEL // LANES
ROW_TILE = 256
MOE_ROWS = 512
DMA_UNROLL = 8
MOD_ROWS = 16
MOD_TILE_N = 1536
ROUTE_LANES = 128
ROUTE_EXPERT_ROW = 8
VMEM_LIMIT_BYTES = 48 * 1024 * 1024
MASKED = -1e30

F32 = jnp.float32
BF16 = jnp.bfloat16


def _params(*semantics):
    return pltpu.CompilerParams(dimension_semantics=semantics, vmem_limit_bytes=VMEM_LIMIT_BYTES)


def _split_bf16(a):
    hi = a.astype(BF16)
    lo = (a - hi.astype(F32)).astype(BF16)
    return hi, lo


def _dot(a, b):
    return jnp.dot(a, b, preferred_element_type=F32)


def _dot_nt(a, b):
    return lax.dot_general(a, b, (((1,), (1,)), ((), ())), preferred_element_type=F32)


def _silu(a):
    return a * jax.nn.sigmoid(a)


def _mod_kernel(c_ref, w_ref, b_ref, o_ref):
    s_hi, s_lo = _split_bf16(_silu(c_ref[...]))
    w_hi, w_lo = _split_bf16(w_ref[0])
    o_ref[0] = _dot(s_hi, w_hi) + _dot(s_hi, w_lo) + _dot(s_lo, w_hi) + b_ref[0]


def _modulation(cvec, ada_w, ada_b):
    depth, d, n = ada_w.shape
    return pl.pallas_call(
        _mod_kernel,
        grid=(depth, n // MOD_TILE_N),
        in_specs=[
            pl.BlockSpec((MOD_ROWS, d), lambda l, j: (0, 0)),
            pl.BlockSpec((1, d, MOD_TILE_N), lambda l, j: (l, 0, j)),
            pl.BlockSpec((1, 1, MOD_TILE_N), lambda l, j: (l, 0, j)),
        ],
        out_specs=pl.BlockSpec((1, MOD_ROWS, MOD_TILE_N), lambda l, j: (l, 0, j)),
        out_shape=jax.ShapeDtypeStruct((depth, MOD_ROWS, n), F32),
        compiler_params=_params("parallel", "parallel"),
        name="adaln_mod",
    )(cvec, ada_w, ada_b.reshape(depth, 1, n))


def _mod_spec(chunk, blocks_per_batch, ctx_row):
    def index(i):
        return (jnp.where(i % blocks_per_batch == 0, ctx_row, i // blocks_per_batch), 0, chunk)
    return pl.BlockSpec((1, 1, D_MODEL), index)


def _store_token_tiles(ref, rows):
    n = rows.shape[0]
    for s in range(TOKEN_SUBLANES):
        ref[pl.ds(s, n, stride=TOKEN_SUBLANES), :] = rows[:, s * LANES:(s + 1) * LANES]


def _load_token_tiles(ref, n, s):
    return ref[pl.ds(s, n, stride=TOKEN_SUBLANES), :]


def _stream_specs(stream):
    if not isinstance(stream, tuple):
        return [pl.BlockSpec((ROW_TILE, D_MODEL), lambda i: (i, 0))], [stream]
    assert CTX_LEN == ROW_TILE
    bpb = (CTX_LEN + SEQ) // ROW_TILE
    xpb = SEQ // ROW_TILE
    ctx_spec = pl.BlockSpec((ROW_TILE, D_MODEL), lambda i: (i // bpb, 0))
    x_spec = pl.BlockSpec((ROW_TILE, D_MODEL), lambda i: ((i // bpb) * xpb + jnp.maximum(i % bpb - 1, 0), 0))
    return [ctx_spec, x_spec], list(stream)


def _stream_rows(refs):
    if len(refs) == 1:
        return refs[0][...]
    is_ctx = pl.program_id(0) % ((CTX_LEN + SEQ) // ROW_TILE) == 0
    return jnp.where(is_ctx, refs[0][...], refs[1][...])


def _stream_len(stream):
    return sum(a.shape[0] for a in stream) if isinstance(stream, tuple) else stream.shape[0]


def _norm_modulate(x, g, shift, scale):
    ms = jnp.mean(x * x, axis=-1, keepdims=True)
    return (x * lax.rsqrt(ms + EPS) * g) * (1.0 + scale) + shift


def _head_mean_square(a):
    first = lax.broadcasted_iota(jnp.int32, (1, LANES), 1) < A_HEAD_DIM
    tiles = []
    for t in range(a.shape[-1] // LANES):
        sq = a[:, t * LANES:(t + 1) * LANES]
        sq = sq * sq
        lo = jnp.sum(jnp.where(first, sq, 0.0), axis=-1, keepdims=True)
        hi = jnp.sum(jnp.where(first, 0.0, sq), axis=-1, keepdims=True)
        tiles.append(jnp.where(first, lo, hi))
    return jnp.concatenate(tiles, axis=-1) * (1.0 / A_HEAD_DIM)


def _qk_norm_rope(a, perm_ref, cos_ref, sin_ref):
    w = a.shape[-1]
    rot = _dot(a.astype(BF16), perm_ref[:w, :w])
    return lax.rsqrt(_head_mean_square(a) + EPS) * (a * cos_ref[...] + rot * sin_ref[...])


def _in_even_kernel(n_x, *refs):
    x_refs = refs[:n_x]
    (g_ref, sh_ref, sc_ref, w_ref, cq_ref, sq_ref, ck_ref, sk_ref, perm_ref,
     q_ref, k_ref, v_ref, u_ref) = refs[n_x:]
    x = _stream_rows(x_refs)
    hb = _norm_modulate(x, g_ref[...], sh_ref[0], sc_ref[0]).astype(BF16)
    c0, c1, c2 = A_WIDTH, A_WIDTH + A_KV_WIDTH, A_WIDTH + 2 * A_KV_WIDTH
    q = _dot(hb, w_ref[:, :c0])
    q_ref[...] = _qk_norm_rope(q, perm_ref, cq_ref, sq_ref).astype(BF16)
    k = _qk_norm_rope(_dot(hb, w_ref[:, c0:c1]), perm_ref, ck_ref, sk_ref)
    v = _dot(hb, w_ref[:, c1:c2])
    ones_col = (lax.broadcasted_iota(jnp.int32, (x.shape[0], LANES - A_HEAD_DIM), 1) == 0).astype(F32)
    for h in range(A_KV_HEADS):
        lanes = slice(h * A_HEAD_DIM, (h + 1) * A_HEAD_DIM)
        k_ref[h] = k[:, lanes].astype(BF16)
        v_ref[h] = jnp.concatenate([v[:, lanes], ones_col], axis=-1).astype(BF16)
    u_ref[...] = _dot(hb, w_ref[:, c2:])


def _rope_tables(q_gain, k_gain):
    t = np.arange(SEQ)
    pos = np.stack([t // GRID_W, t % GRID_W], axis=-1).astype(np.float32)
    inv = (ROPE_THETA ** (-np.arange(ROPE_FREQS, dtype=np.float32) / ROPE_FREQS)).astype(np.float32)
    ang = pos[:, :, None] * inv
    cos = np.concatenate([np.cos(ang), np.cos(ang)], axis=-1).reshape(SEQ, A_HEAD_DIM)
    sin = np.concatenate([-np.sin(ang), np.sin(ang)], axis=-1).reshape(SEQ, A_HEAD_DIM)
    cos = np.concatenate([np.ones((CTX_LEN, A_HEAD_DIM), np.float32), cos.astype(np.float32)], axis=0)
    sin = np.concatenate([np.zeros((CTX_LEN, A_HEAD_DIM), np.float32), sin.astype(np.float32)], axis=0)
    swap = _rope_swap()
    def tables(gain, heads, scale):
        c = jnp.asarray(cos) * (gain * scale)[None, :]
        s = jnp.asarray(sin) * (gain[swap] * scale)[None, :]
        return jnp.tile(c, (1, heads)), jnp.tile(s, (1, heads))
    cq, sq = tables(q_gain, A_HEADS, A_SCALE * LOG2_E)
    ck, sk = tables(k_gain, A_KV_HEADS, 1.0)
    return cq, sq, ck, sk


def _rope_swap():
    j = np.arange(A_HEAD_DIM)
    half = (j // ROPE_FREQS) % 2
    return np.where(half == 0, j + ROPE_FREQS, j - ROPE_FREQS)


def _rotate_half_matrix():
    lane = np.arange(A_WIDTH)
    swap = np.tile(_rope_swap(), A_HEADS) + (lane // A_HEAD_DIM) * A_HEAD_DIM
    perm = np.zeros((A_WIDTH, A_WIDTH), np.float32)
    perm[swap, lane] = 1.0
    return jnp.asarray(perm, BF16)


def _in_even(stream, mod_l, norm_g, w_in, q_gain, k_gain):
    t_rows = _stream_len(stream)
    x_specs, x_args = _stream_specs(stream)
    bpb = (CTX_LEN + SEQ) // ROW_TILE
    cq, sq, ck, sk = _rope_tables(q_gain, k_gain)
    n_in = w_in.shape[1]
    row = lambda w: pl.BlockSpec((ROW_TILE, w), lambda i: (i, 0))
    tab = lambda w: pl.BlockSpec((ROW_TILE, w), lambda i: (i % bpb, 0))
    full = lambda a, b: pl.BlockSpec((a, b), lambda i: (0, 0))
    return pl.pallas_call(
        functools.partial(_in_even_kernel, len(x_specs)),
        grid=(t_rows // ROW_TILE,),
        in_specs=x_specs + [
            full(1, D_MODEL), _mod_spec(0, bpb, BATCH), _mod_spec(1, bpb, BATCH),
            full(D_MODEL, n_in), tab(A_WIDTH), tab(A_WIDTH), tab(A_KV_WIDTH), tab(A_KV_WIDTH),
            full(A_WIDTH, A_WIDTH),
        ],
        out_specs=[row(A_WIDTH),
                   pl.BlockSpec((A_KV_HEADS, ROW_TILE, A_HEAD_DIM), lambda i: (0, i, 0)),
                   pl.BlockSpec((A_KV_HEADS, ROW_TILE, LANES), lambda i: (0, i, 0)),
                   row(B_WIDTH)],
        out_shape=[
            jax.ShapeDtypeStruct((t_rows, A_WIDTH), BF16),
            jax.ShapeDtypeStruct((A_KV_HEADS, t_rows, A_HEAD_DIM), BF16),
            jax.ShapeDtypeStruct((A_KV_HEADS, t_rows, LANES), BF16),
            jax.ShapeDtypeStruct((t_rows, B_WIDTH), F32),
        ],
        compiler_params=_params("parallel"),
        name="even_in_proj",
    )(*x_args, norm_g.reshape(1, D_MODEL), mod_l, mod_l, w_in.astype(BF16), cq, sq, ck, sk, _rotate_half_matrix())


def _gqa_kernel(q_ref, k_ref, v_ref, o_ref):
    def attend(n_keys):
        k = k_ref[0, :n_keys, :]
        v = v_ref[0, :n_keys, :]
        outs = []
        for h in range(A_HEADS // A_KV_HEADS):
            s = _dot_nt(q_ref[:, h * A_HEAD_DIM:(h + 1) * A_HEAD_DIM], k)
            m = jnp.max(s, axis=-1, keepdims=True)
            p = jnp.exp2(s - m).astype(BF16)
            acc = _dot(p, v)
            outs.append(acc[:, :A_HEAD_DIM] / acc[:, A_HEAD_DIM:A_HEAD_DIM + 1])
        o_ref[...] = jnp.concatenate(outs, axis=-1).astype(BF16)

    is_ctx = pl.program_id(2) == 0

    @pl.when(is_ctx)
    def _():
        attend(CTX_LEN)

    @pl.when(jnp.logical_not(is_ctx))
    def _():
        attend(CTX_LEN + SEQ)


def _gqa(q, k_heads, v_heads):
    t_rows = q.shape[0]
    r = CTX_LEN + SEQ
    bpb = r // ROW_TILE
    group_w = (A_HEADS // A_KV_HEADS) * A_HEAD_DIM
    return pl.pallas_call(
        _gqa_kernel,
        grid=(BATCH, A_KV_HEADS, bpb),
        in_specs=[
            pl.BlockSpec((ROW_TILE, group_w), lambda b, g, j: (b * bpb + j, g)),
            pl.BlockSpec((1, r, A_HEAD_DIM), lambda b, g, j: (g, b, 0)),
            pl.BlockSpec((1, r, LANES), lambda b, g, j: (g, b, 0)),
        ],
        out_specs=pl.BlockSpec((ROW_TILE, group_w), lambda b, g, j: (b * bpb + j, g)),
        out_shape=jax.ShapeDtypeStruct((t_rows, A_WIDTH), BF16),
        compiler_params=_params("parallel", "parallel", "parallel"),
        name="gqa_attention",
    )(q, k_heads, v_heads)


def _pool_kernel(u_ref, w_ref, s_ref, o_ref, pad_ref):
    g = pl.program_id(1)
    w_bf = w_ref[0].astype(BF16)
    for start, length in ((0, CTX_LEN), (CTX_LEN, SEQ)):
        u = u_ref[start:start + length, :]
        pad_ref[0:POOL_PAD, :] = jnp.zeros((POOL_PAD, LANES), F32)
        pad_ref[POOL_PAD:POOL_PAD + length, :] = u
        pad_ref[POOL_PAD + length:2 * POOL_PAD + length, :] = jnp.zeros((POOL_PAD, LANES), F32)
        t = lax.broadcasted_iota(jnp.int32, (length, LANES), 0)
        for gi, win in enumerate(POOL_WINDOWS):
            @pl.when(g == gi)
            def _(win=win, u=u, t=t, start=start, length=length):
                base = POOL_PAD - win // 2
                acc = pad_ref[base:base + length, :]
                for j in range(1, win):
                    acc = acc + pad_ref[base + j:base + j + length, :]
                lo = jnp.clip(t - win // 2, 0, length)
                hi = jnp.clip(t - win // 2 + win, 0, length)
                pooled = acc / (hi - lo).astype(F32) - u
                mixed = _dot(pooled.astype(BF16), w_bf) * s_ref[0]
                o_ref[start:start + length, :] = mixed.astype(BF16)


def _pool(u, pool_w, pool_scale):
    t_rows = u.shape[0]
    r = CTX_LEN + SEQ
    return pl.pallas_call(
        _pool_kernel,
        grid=(BATCH, B_GROUPS),
        in_specs=[
            pl.BlockSpec((r, B_GROUP_DIM), lambda b, g: (b, g)),
            pl.BlockSpec((1, B_GROUP_DIM, B_GROUP_DIM), lambda b, g: (g, 0, 0)),
            pl.BlockSpec((1, 1, B_GROUP_DIM), lambda b, g: (g, 0, 0)),
        ],
        out_specs=pl.BlockSpec((r, B_GROUP_DIM), lambda b, g: (b, g)),
        out_shape=jax.ShapeDtypeStruct((t_rows, B_WIDTH), BF16),
        scratch_shapes=[pltpu.VMEM((SEQ + 2 * POOL_PAD, B_GROUP_DIM), F32)],
        compiler_params=_params("parallel", "parallel"),
        name="pool_mixer",
    )(u, pool_w, pool_scale.reshape(B_GROUPS, 1, B_GROUP_DIM))


def _in_odd_kernel(x_ref, g_ref, sh_ref, sc_ref, w_ref, q_ref, k_ref, v_ref):
    hb = _norm_modulate(x_ref[...], g_ref[...], sh_ref[0], sc_ref[0]).astype(BF16)
    q_ref[...] = (_dot(hb, w_ref[:, :C_WIDTH]) * (C_SCALE * LOG2_E)).astype(BF16)
    k_ref[...] = _dot(hb, w_ref[:, C_WIDTH:2 * C_WIDTH]).astype(BF16)
    v_ref[...] = _dot(hb, w_ref[:, 2 * C_WIDTH:]).astype(BF16)


def _in_odd(xs, mod_l, norm_g, w_in):
    t_rows = xs.shape[0]
    bpb = (CTX_LEN + SEQ) // ROW_TILE
    row = lambda w: pl.BlockSpec((ROW_TILE, w), lambda i: (i, 0))
    full = lambda a, b: pl.BlockSpec((a, b), lambda i: (0, 0))
    return pl.pallas_call(
        _in_odd_kernel,
        grid=(t_rows // ROW_TILE,),
        in_specs=[row(D_MODEL), full(1, D_MODEL), _mod_spec(0, bpb, BATCH), _mod_spec(1, bpb, BATCH),
                  full(D_MODEL, 3 * C_WIDTH)],
        out_specs=[row(C_WIDTH)] * 3,
        out_shape=[jax.ShapeDtypeStruct((t_rows, C_WIDTH), BF16)] * 3,
        compiler_params=_params("parallel"),
        name="odd_in_proj",
    )(xs, norm_g.reshape(1, D_MODEL), mod_l, mod_l, w_in.astype(BF16))


def _na_plan():
    rows = SEQ // GRID_W
    kh = min(NA_ROWS_MAX, rows)
    win_rows = min(kh + NA_BAND - 1, rows)
    starts, types, patterns = [], [], []
    for r0 in range(0, rows, NA_BAND):
        ws = int(np.clip(r0 - kh // 2, 0, rows - win_rows))
        pat = np.full((NA_BAND, win_rows), -1, np.int64)
        for rr in range(NA_BAND):
            r = r0 + rr
            rs = int(np.clip(r - kh // 2, 0, rows - kh))
            for wi in range(win_rows):
                if rs <= ws + wi < rs + kh:
                    pat[rr, wi] = ws + wi - r + (NA_ROWS_MAX - 1)
        for t, known in enumerate(patterns):
            if np.array_equal(known, pat):
                types.append(t)
                break
        else:
            types.append(len(patterns))
            patterns.append(pat)
        starts.append(ws)
    return win_rows, np.asarray(starts, np.int32), np.asarray(types, np.int32), patterns


def _na_strip_plan(patterns):
    n_off = 2 * NA_ROWS_MAX - 1
    cuts = []
    for pat in patterns:
        rows = []
        for rr in range(pat.shape[0]):
            valid = np.nonzero(pat[rr] >= 0)[0]
            lo, hi = int(valid[0]), int(valid[-1]) + 1
            delta = int(pat[rr, lo]) - lo
            assert np.array_equal(pat[rr, lo:hi], np.arange(lo, hi) + delta)
            rows.append((delta, lo, hi))
        cuts.append(rows)
    win_rows = patterns[0].shape[1]
    deltas = [d for rows in cuts for d, _, _ in rows]
    pad_lo = max(0, -min(deltas))
    pad_hi = max(0, max(deltas) + win_rows - n_off)
    pad_hi += (pad_lo + n_off + pad_hi) % 2
    plan = tuple(tuple((d + pad_lo, lo, hi) for d, lo, hi in rows) for rows in cuts)
    return plan, pad_lo, pad_hi


def _na_bias_strip(rel_bias, pad_lo, pad_hi):
    cq = np.arange(GRID_W)
    col_start = np.clip(cq - NA_COLS // 2, 0, GRID_W - NA_COLS)
    ck = np.arange(GRID_W)
    inside = (ck[None, :] >= col_start[:, None]) & (ck[None, :] < col_start[:, None] + NA_COLS)
    shift = GRID_W - NA_COLS
    padded = jnp.pad(rel_bias.astype(F32) * LOG2_E, ((0, 0), (0, 0), (shift, shift)))
    tcol = jnp.stack([padded[:, :, GRID_W - 1 - c:2 * GRID_W - 1 - c] for c in range(GRID_W)], axis=1)
    tcol = jnp.where(jnp.asarray(inside)[None, :, None, :], tcol, MASKED)
    strip = tcol.reshape(C_HEADS, GRID_W, -1)
    return jnp.pad(strip, ((0, 0), (0, 0), (pad_lo * GRID_W, pad_hi * GRID_W)), constant_values=MASKED)


def _na_kernel(plan, ws_ref, ty_ref, q_ref, k_ref, v_ref, strip_ref, o_ref, bias_ref):
    n_bands = (SEQ // GRID_W) // NA_BAND
    band_q = NA_BAND * GRID_W
    win = bias_ref.shape[-1]

    @pl.when(pl.program_id(1) == 0)
    def _():
        win_row = lax.broadcasted_iota(jnp.int32, (GRID_W, win), 1) // GRID_W
        for h in range(2):
            for t, rows in enumerate(plan):
                for rr, (block, lo, hi) in enumerate(rows):
                    cut = strip_ref[h, :, block * GRID_W:block * GRID_W + win]
                    own = jnp.logical_and(win_row >= lo, win_row < hi)
                    bias_ref[h, t, rr * GRID_W:(rr + 1) * GRID_W, :] = jnp.where(own, cut, MASKED)

    o_ref[0:CTX_LEN, :] = jnp.zeros((CTX_LEN, LANES), BF16)
    kc = k_ref[0:CTX_LEN, :]
    vc = v_ref[0:CTX_LEN, :]
    first = lax.broadcasted_iota(jnp.int32, (1, LANES), 1) < C_HEAD_DIM

    def one_band(b, carry):
        q_off = pl.multiple_of(CTX_LEN + b * band_q, GRID_W)
        k_off = pl.multiple_of(CTX_LEN + ws_ref[b] * GRID_W, GRID_W)
        qb = q_ref[pl.ds(q_off, band_q), :]
        kw = k_ref[pl.ds(k_off, win), :]
        vw = v_ref[pl.ds(k_off, win), :]
        outs = []
        for h in range(2):
            qm = jnp.where(first if h == 0 else jnp.logical_not(first), qb, jnp.zeros_like(qb))
            s_loc = _dot_nt(qm, kw) + bias_ref[h, ty_ref[b]]
            s_ctx = _dot_nt(qm, kc)
            m = jnp.maximum(jnp.max(s_loc, axis=-1, keepdims=True), jnp.max(s_ctx, axis=-1, keepdims=True))
            p_loc = jnp.exp2(s_loc - m)
            p_ctx = jnp.exp2(s_ctx - m)
            denom = jnp.sum(p_loc, axis=-1, keepdims=True) + jnp.sum(p_ctx, axis=-1, keepdims=True)
            o = _dot(p_loc.astype(BF16), vw) + _dot(p_ctx.astype(BF16), vc)
            outs.append(o / denom)
        o_ref[pl.ds(q_off, band_q), :] = jnp.where(first, outs[0], outs[1]).astype(BF16)
        return carry

    lax.fori_loop(0, n_bands, one_band, 0, unroll=2)


def _na(q, k, v, rel_bias):
    t_rows = q.shape[0]
    r = CTX_LEN + SEQ
    win_rows, starts, types, patterns = _na_plan()
    plan, pad_lo, pad_hi = _na_strip_plan(patterns)
    strip = _na_bias_strip(rel_bias, pad_lo, pad_hi)
    blk = pl.BlockSpec((r, LANES), lambda p, b, ws, ty: (b, p))
    return pl.pallas_call(
        functools.partial(_na_kernel, plan),
        grid_spec=pltpu.PrefetchScalarGridSpec(
            num_scalar_prefetch=2,
            grid=(C_HEADS // 2, BATCH),
            in_specs=[blk, blk, blk,
                      pl.BlockSpec((2, GRID_W, strip.shape[-1]), lambda p, b, ws, ty: (p, 0, 0))],
            out_specs=blk,
            scratch_shapes=[pltpu.VMEM((2, len(patterns), NA_BAND * GRID_W, win_rows * GRID_W), F32)],
        ),
        out_shape=jax.ShapeDtypeStruct((t_rows, C_WIDTH), BF16),
        compiler_params=_params("parallel", "arbitrary"),
        name="neighbourhood_attention",
    )(jnp.asarray(starts), jnp.asarray(types), q, k, v, strip)


def _out_kernel(n_lat, n_x, *refs):
    lat_refs = refs[:n_lat]
    w_ref = refs[n_lat]
    x_refs = refs[n_lat + 1:n_lat + 1 + n_x]
    g1_ref, g_ref, sh_ref, sc_ref, wr_ref, br_ref, xo_ref, h_ref, lg_ref = refs[n_lat + 1 + n_x:]
    y = None
    off = 0
    for lat in lat_refs:
        width = lat.shape[-1]
        part = _dot(lat[...], w_ref[off:off + width, :])
        y = part if y is None else y + part
        off += width
    x_new = _stream_rows(x_refs) + g1_ref[0] * y
    xo_ref[...] = x_new
    h = _norm_modulate(x_new, g_ref[...], sh_ref[0], sc_ref[0])
    _store_token_tiles(h_ref, h)
    h_hi, h_lo = _split_bf16(h)
    both = _dot(h_hi, wr_ref[...])
    lg_ref[...] = (both[:, :ROUTE_LANES] + both[:, ROUTE_LANES:]
                   + _dot(h_lo, wr_ref[:, :ROUTE_LANES]) + br_ref[...])


def _router_weights(wg, bg, we, be):
    gap = ROUTE_EXPERT_ROW - N_GROUPS
    tail = ROUTE_LANES - ROUTE_EXPERT_ROW - N_EXPERTS
    d = wg.shape[0]
    w = jnp.concatenate([wg, jnp.zeros((d, gap), F32), we, jnp.zeros((d, tail), F32)], axis=1)
    w_hi, w_lo = _split_bf16(w)
    b = jnp.concatenate([bg, jnp.zeros((gap,), F32), be, jnp.zeros((tail,), F32)])
    return jnp.concatenate([w_hi, w_lo], axis=1), b.reshape(1, ROUTE_LANES)


def _out_proj(lats, w_out, stream, mod_l, norm_g, wr, br):
    t_rows = _stream_len(stream)
    x_specs, x_args = _stream_specs(stream)
    bpb = (CTX_LEN + SEQ) // ROW_TILE
    row = lambda w: pl.BlockSpec((ROW_TILE, w), lambda i: (i, 0))
    full = lambda a, b: pl.BlockSpec((a, b), lambda i: (0, 0))
    return pl.pallas_call(
        functools.partial(_out_kernel, len(lats), len(x_specs)),
        grid=(t_rows // ROW_TILE,),
        in_specs=[row(l.shape[1]) for l in lats] + [full(*w_out.shape)] + x_specs + [
            _mod_spec(2, bpb, BATCH), full(1, D_MODEL),
            _mod_spec(3, bpb, BATCH), _mod_spec(4, bpb, BATCH), full(D_MODEL, 2 * ROUTE_LANES), full(1, ROUTE_LANES),
        ],
        out_specs=[row(D_MODEL), pl.BlockSpec((ROW_TILE * TOKEN_SUBLANES, LANES), lambda i: (i, 0)),
                   row(ROUTE_LANES)],
        out_shape=[
            jax.ShapeDtypeStruct((t_rows, D_MODEL), F32),
            jax.ShapeDtypeStruct((t_rows * TOKEN_SUBLANES, LANES), F32),
            jax.ShapeDtypeStruct((t_rows, ROUTE_LANES), F32),
        ],
        compiler_params=_params("parallel"),
        name="out_proj_ffn_norm",
    )(*lats, w_out.astype(BF16), *x_args, mod_l, norm_g.reshape(1, D_MODEL), mod_l, mod_l, wr, br)


def _route_kernel(lg_ref, tri_ref, e_ref, rk_ref, wt_ref, cnt_ref, run_ref):
    @pl.when(pl.program_id(0) == 0)
    def _():
        run_ref[...] = jnp.zeros(run_ref.shape, F32)

    lt = lg_ref[...].T
    sub = lax.broadcasted_iota(jnp.int32, (E_PER_GROUP, ROW_TILE), 0)
    gl = jnp.where(sub < N_GROUPS, lt[0:E_PER_GROUP], -jnp.inf)
    g_max = jnp.max(gl, axis=0, keepdims=True)
    g_idx = jnp.min(jnp.where(gl == g_max, sub, E_PER_GROUP), axis=0, keepdims=True)
    g_w = 1.0 / jnp.sum(jnp.exp(gl - g_max), axis=0, keepdims=True)
    sel = lt[ROUTE_EXPERT_ROW:ROUTE_EXPERT_ROW + E_PER_GROUP]
    for g in range(1, N_GROUPS):
        lo = ROUTE_EXPERT_ROW + g * E_PER_GROUP
        sel = jnp.where(g_idx == g, lt[lo:lo + E_PER_GROUP], sel)
    m1 = jnp.max(sel, axis=0, keepdims=True)
    i1 = jnp.min(jnp.where(sel == m1, sub, E_PER_GROUP), axis=0, keepdims=True)
    rest = jnp.where(sub == i1, -jnp.inf, sel)
    m2 = jnp.max(rest, axis=0, keepdims=True)
    i2 = jnp.min(jnp.where(rest == m2, sub, E_PER_GROUP), axis=0, keepdims=True)
    t = jnp.exp(m2 - m1)
    p1 = 1.0 / (1.0 + t)
    wt_ref[0] = jnp.concatenate([g_w * p1, g_w * (t * p1)], axis=0)
    e_flat = jnp.concatenate([g_idx * E_PER_GROUP + i1, g_idx * E_PER_GROUP + i2], axis=1)
    e_ref[0] = e_flat
    hit = lax.broadcasted_iota(jnp.int32, (N_EXPERTS, TOP_K * ROW_TILE), 0) == e_flat
    onehot = jnp.where(hit, 1.0, 0.0)
    before = _dot(onehot.astype(BF16), tri_ref[...]) + run_ref[:, 0:1]
    rk_ref[0] = jnp.sum(jnp.where(hit, before, 0.0), axis=0, keepdims=True).astype(jnp.int32)
    run_ref[...] = run_ref[...] + jnp.sum(onehot, axis=1, keepdims=True)
    cnt_ref[...] = run_ref[...]


def _route(logits):
    t_rows = logits.shape[0]
    n_blk = t_rows // ROW_TILE
    n_asg = TOP_K * ROW_TILE
    tri = jnp.asarray(np.triu(np.ones((n_asg, n_asg), np.float32), 1), BF16)
    e_blk, rank_blk, wt_blk, counts = pl.pallas_call(
        _route_kernel,
        grid=(n_blk,),
        in_specs=[pl.BlockSpec((ROW_TILE, ROUTE_LANES), lambda i: (i, 0)),
                  pl.BlockSpec((n_asg, n_asg), lambda i: (0, 0))],
        out_specs=[pl.BlockSpec((1, 1, n_asg), lambda i: (i, 0, 0)),
                   pl.BlockSpec((1, 1, n_asg), lambda i: (i, 0, 0)),
                   pl.BlockSpec((1, TOP_K, ROW_TILE), lambda i: (i, 0, 0)),
                   pl.BlockSpec((N_EXPERTS, LANES), lambda i: (0, 0))],
        out_shape=[jax.ShapeDtypeStruct((n_blk, 1, n_asg), jnp.int32),
                   jax.ShapeDtypeStruct((n_blk, 1, n_asg), jnp.int32),
                   jax.ShapeDtypeStruct((n_blk, TOP_K, ROW_TILE), F32),
                   jax.ShapeDtypeStruct((N_EXPERTS, LANES), F32)],
        scratch_shapes=[pltpu.VMEM((N_EXPERTS, LANES), F32)],
        compiler_params=_params("arbitrary"),
        name="moe_route",
    )(logits, tri)
    counts = counts[:, 0].astype(jnp.int32)
    blocks_of = (counts + MOE_ROWS - 1) // MOE_ROWS
    b_end = jnp.cumsum(blocks_of)
    slot_base = ((b_end - blocks_of) * MOE_ROWS).astype(jnp.int32)
    n_used = b_end[-1].astype(jnp.int32)
    blk = jnp.minimum(jnp.arange(_moe_blocks(t_rows), dtype=jnp.int32), n_used - 1)
    block_e = jnp.minimum(jnp.sum((b_end[None, :] <= blk[:, None]).astype(jnp.int32), axis=1), N_EXPERTS - 1)
    weights = wt_blk.transpose(0, 2, 1).reshape(t_rows, TOP_K)
    return (slot_base, counts, e_blk, rank_blk), weights, block_e, n_used.reshape(1)


def _moe_blocks(t_rows):
    return (t_rows * TOP_K) // MOE_ROWS + N_EXPERTS


def _tile_copy(src_ref, src_token, dst_ref, dst_token, sem):
    src = src_ref.at[pl.ds(pl.multiple_of(src_token * TOKEN_SUBLANES, TOKEN_SUBLANES), TOKEN_SUBLANES)]
    dst = dst_ref.at[pl.ds(pl.multiple_of(dst_token * TOKEN_SUBLANES, TOKEN_SUBLANES), TOKEN_SUBLANES)]
    return pltpu.make_async_copy(src, dst, sem)


def _start_rows(copies):
    def start(r, c):
        for n, cp in enumerate(copies(r)):
            cp.start(priority=n % 2)
        return c

    lax.fori_loop(0, ROW_TILE, start, 0, unroll=DMA_UNROLL)


def _wait_rows(copies):
    def wait(r, c):
        for cp in copies(r):
            cp.wait()
        return c

    lax.fori_loop(0, ROW_TILE, wait, 0, unroll=DMA_UNROLL)


def _start_then_wait(copies):
    _start_rows(copies)
    _wait_rows(copies)


def _slot(base_ref, e_ref, rk_ref, j):
    return base_ref[e_ref[0, 0, j]] + rk_ref[0, 0, j]


def _dispatch_kernel(base_ref, cnt_ref, e_ref, rk_ref, h_ref, xpad_ref, zero_ref, sem):
    _start_then_wait(lambda r: [
        _tile_copy(h_ref, r, xpad_ref, _slot(base_ref, e_ref, rk_ref, k * ROW_TILE + r), sem)
        for k in range(TOP_K)])

    @pl.when(pl.program_id(0) == pl.num_programs(0) - 1)
    def _():
        zero_ref[...] = jnp.zeros(zero_ref.shape, F32)
        block_end = lambda e: base_ref[e] + (cnt_ref[e] + MOE_ROWS - 1) // MOE_ROWS * MOE_ROWS

        def pad_expert(act, e, c):
            lo = base_ref[e] + cnt_ref[e]
            n = block_end(e) - lo
            size = MOE_ROWS // 2
            while size >= 1:
                at = pl.multiple_of((lo + (n & ~(2 * size - 1))) * TOKEN_SUBLANES, TOKEN_SUBLANES)
                cp = pltpu.make_async_copy(zero_ref.at[pl.ds(0, size * TOKEN_SUBLANES)],
                                           xpad_ref.at[pl.ds(at, size * TOKEN_SUBLANES)], sem)
                pl.when((n & size) != 0)(functools.partial(act, cp))
                size //= 2
            return c

        lax.fori_loop(0, N_EXPERTS, functools.partial(pad_expert, lambda cp: cp.start()), 0)
        lax.fori_loop(0, N_EXPERTS, functools.partial(pad_expert, lambda cp: cp.wait()), 0)
        block_rows = MOE_ROWS * TOKEN_SUBLANES
        first_unused = block_end(N_EXPERTS - 1) // MOE_ROWS
        n_blocks = xpad_ref.shape[0] // block_rows
        block_copy = lambda b: pltpu.make_async_copy(
            zero_ref, xpad_ref.at[pl.ds(pl.multiple_of(b * block_rows, block_rows), block_rows)], sem)
        lax.fori_loop(first_unused, n_blocks, lambda b, cc: (block_copy(b).start(), cc)[1], 0)
        lax.fori_loop(first_unused, n_blocks, lambda b, cc: (block_copy(b).wait(), cc)[1], 0)


def _dispatch(h_tiles, slots):
    slot_base, counts, e_blk, rank_blk = slots
    t_rows = h_tiles.shape[0] // TOKEN_SUBLANES
    n_pad = _moe_blocks(t_rows) * MOE_ROWS
    asg = pl.BlockSpec((1, 1, TOP_K * ROW_TILE), lambda i, base, cnt: (i, 0, 0), memory_space=pltpu.SMEM)
    return pl.pallas_call(
        _dispatch_kernel,
        grid_spec=pltpu.PrefetchScalarGridSpec(
            num_scalar_prefetch=2,
            grid=(t_rows // ROW_TILE,),
            in_specs=[asg, asg,
                      pl.BlockSpec((ROW_TILE * TOKEN_SUBLANES, LANES), lambda i, base, cnt: (i, 0))],
            out_specs=pl.BlockSpec(memory_space=pl.ANY),
            scratch_shapes=[pltpu.VMEM((MOE_ROWS * TOKEN_SUBLANES, LANES), F32), pltpu.SemaphoreType.DMA(())],
        ),
        out_shape=jax.ShapeDtypeStruct((n_pad * TOKEN_SUBLANES, LANES), F32),
        compiler_params=_params("arbitrary"),
        name="moe_dispatch",
    )(slot_base, counts, e_blk, rank_blk, h_tiles)


def _expert_kernel(be_ref, nu_ref, x_ref, w1_ref, w3_ref, w2_ref, y_ref, w1_bf, w3_bf, w2_bf):
    i = pl.program_id(0)
    fresh = jnp.logical_or(i == 0, be_ref[i] != be_ref[jnp.maximum(i - 1, 0)])

    @pl.when(fresh)
    def _():
        w1_bf[...] = w1_ref[0, 0].astype(BF16)
        w3_bf[...] = w3_ref[0, 0].astype(BF16)
        w2_bf[...] = w2_ref[0, 0].astype(BF16)

    used = i < nu_ref[0]

    @pl.when(used)
    def _():
        xb = jnp.concatenate(
            [_load_token_tiles(x_ref, MOE_ROWS, s).astype(BF16) for s in range(TOKEN_SUBLANES)], axis=-1)
        hid = _silu(_dot(xb, w1_bf[...])) * _dot(xb, w3_bf[...])
        _store_token_tiles(y_ref, _dot(hid.astype(BF16), w2_bf[...]))

    @pl.when(jnp.logical_not(used))
    def _():
        y_ref[...] = jnp.zeros(y_ref.shape, F32)


def _experts(xpad, block_e, n_used, w1, w3, w2, layer):
    g_blocks = xpad.shape[0] // (MOE_ROWS * TOKEN_SUBLANES)
    tile_rows = MOE_ROWS * TOKEN_SUBLANES
    x_index = lambda i, be, nu: (jnp.minimum(i, nu[0] - 1), 0)
    w_index = lambda i, be, nu: (layer, be[i], 0, 0)
    return pl.pallas_call(
        _expert_kernel,
        grid_spec=pltpu.PrefetchScalarGridSpec(
            num_scalar_prefetch=2,
            grid=(g_blocks,),
            in_specs=[
                pl.BlockSpec((tile_rows, LANES), x_index),
                pl.BlockSpec((1, 1, D_MODEL, D_EXPERT), w_index),
                pl.BlockSpec((1, 1, D_MODEL, D_EXPERT), w_index),
                pl.BlockSpec((1, 1, D_EXPERT, D_MODEL), w_index),
            ],
            out_specs=pl.BlockSpec((tile_rows, LANES), lambda i, be, nu: (i, 0)),
            scratch_shapes=[
                pltpu.VMEM((D_MODEL, D_EXPERT), BF16),
                pltpu.VMEM((D_MODEL, D_EXPERT), BF16),
                pltpu.VMEM((D_EXPERT, D_MODEL), BF16),
            ],
        ),
        out_shape=jax.ShapeDtypeStruct((g_blocks * tile_rows, LANES), F32),
        compiler_params=_params("arbitrary"),
        name="moe_experts",
    )(block_e, n_used, xpad, w1, w3, w2)


def _combine_kernel(final, base_ref, e_ref, rk_ref, e_next, rk_next, x_ref, wt_ref, g2_ref, fg_ref, ypad_ref,
                    o_ref, ybuf, sems):
    i = pl.program_id(0)
    slot = i % 2

    def gathers(e, rk, into):
        return lambda r: [
            _tile_copy(ypad_ref, _slot(base_ref, e, rk, k * ROW_TILE + r), ybuf.at[into, k], r, sems.at[into])
            for k in range(TOP_K)]

    @pl.when(i == 0)
    def _():
        _start_rows(gathers(e_ref, rk_ref, slot))

    @pl.when(i + 1 < pl.num_programs(0))
    def _():
        _start_rows(gathers(e_next, rk_next, 1 - slot))

    _wait_rows(gathers(e_ref, rk_ref, slot))
    wt = wt_ref[...]
    parts = []
    for s in range(TOKEN_SUBLANES):
        lanes = slice(s * LANES, (s + 1) * LANES)
        f = (wt[:, 0:1] * _load_token_tiles(ybuf.at[slot, 0], ROW_TILE, s)
             + wt[:, 1:2] * _load_token_tiles(ybuf.at[slot, 1], ROW_TILE, s))
        parts.append(x_ref[:, lanes] + g2_ref[0][:, lanes] * f)
    x_new = jnp.concatenate(parts, axis=-1)
    if final:
        ms = jnp.mean(x_new * x_new, axis=-1, keepdims=True)
        x_new = x_new * lax.rsqrt(ms + EPS) * fg_ref[...]
    o_ref[...] = x_new


def _combine(ypad, slots, weights, xs, mod_l, final_g, final):
    slot_base, _, e_blk, rank_blk = slots
    t_rows = xs.shape[0]
    bpb = (CTX_LEN + SEQ) // ROW_TILE
    if final:
        xpb = SEQ // ROW_TILE
        n_out = BATCH * xpb
        blk = lambda i: (i // xpb) * bpb + i % xpb + 1
        g2_index = lambda i, base: (i // xpb, 0, 5)
    else:
        n_out = t_rows // ROW_TILE
        blk = lambda i: i
        g2_index = lambda i, base: (jnp.where(i % bpb == 0, BATCH, i // bpb), 0, 5)
    asg = pl.BlockSpec((1, 1, TOP_K * ROW_TILE), lambda i, base: (blk(i), 0, 0), memory_space=pltpu.SMEM)
    asg_next = pl.BlockSpec((1, 1, TOP_K * ROW_TILE), lambda i, base: (blk(jnp.minimum(i + 1, n_out - 1)), 0, 0),
                            memory_space=pltpu.SMEM)
    return pl.pallas_call(
        functools.partial(_combine_kernel, final),
        grid_spec=pltpu.PrefetchScalarGridSpec(
            num_scalar_prefetch=1,
            grid=(n_out,),
            in_specs=[
                asg, asg, asg_next, asg_next,
                pl.BlockSpec((ROW_TILE, D_MODEL), lambda i, base: (blk(i), 0)),
                pl.BlockSpec((ROW_TILE, TOP_K), lambda i, base: (blk(i), 0)),
                pl.BlockSpec((1, 1, D_MODEL), g2_index),
                pl.BlockSpec((1, D_MODEL), lambda i, base: (0, 0)),
                pl.BlockSpec(memory_space=pl.ANY),
            ],
            out_specs=pl.BlockSpec((ROW_TILE, D_MODEL), lambda i, base: (i, 0)),
            scratch_shapes=[pltpu.VMEM((2, TOP_K, ROW_TILE * TOKEN_SUBLANES, LANES), F32),
                            pltpu.SemaphoreType.DMA((2,))],
        ),
        out_shape=jax.ShapeDtypeStruct((n_out * ROW_TILE, D_MODEL), F32),
        compiler_params=_params("arbitrary"),
        name="moe_combine_final" if final else "moe_combine",
    )(slot_base, e_blk, rank_blk, e_blk, rank_blk, xs, weights, mod_l, final_g.reshape(1, D_MODEL), ypad)


def _moe(h_tiles, logits, xs, mod_l, w1, w3, w2, layer, final_g, final):
    slots, weights, block_e, n_used = _route(logits)
    xpad = _dispatch(h_tiles, slots)
    ypad = _experts(xpad, block_e, n_used, w1, w3, w2, layer)
    return _combine(ypad, slots, weights, xs, mod_l, final_g, final)


def kernel(x, c, ctx, c_ctx, ada_w, ada_b, norm_mix_g, norm_ffn_g, even_w_in, even_w_out, a_q_gain, a_k_gain, pool_w, pool_scale, odd_w_in, odd_w_out, na_rel_bias, moe_w_group, moe_b_group, moe_w_expert, moe_b_expert, moe_w1, moe_w3, moe_w2, final_g):
    xs = (ctx.reshape(BATCH * CTX_LEN, D_MODEL), x.reshape(BATCH * SEQ, D_MODEL))
    cvec = jnp.concatenate([c, c_ctx[None, :], jnp.zeros((MOD_ROWS - BATCH - 1, D_MODEL), F32)], axis=0)
    mod = _modulation(cvec, ada_w, ada_b)
    for l in range(DEPTH):
        mod_l = mod[l].reshape(MOD_ROWS, 1, N_MOD * D_MODEL)
        i = l // 2
        if l % 2 == 0:
            q, k, v, u = _in_even(xs, mod_l, norm_mix_g[l], even_w_in[i], a_q_gain[i], a_k_gain[i])
            lats = [_gqa(q, k, v), _pool(u, pool_w[i], pool_scale[i])]
            w_out = even_w_out[i]
        else:
            q, k, v = _in_odd(xs, mod_l, norm_mix_g[l], odd_w_in[i])
            lats = [_na(q, k, v, na_rel_bias[i])]
            w_out = odd_w_out[i]
        wr, br = _router_weights(moe_w_group[l], moe_b_group[l], moe_w_expert[l], moe_b_expert[l])
        xs, h, logits = _out_proj(lats, w_out, xs, mod_l, norm_ffn_g[l], wr, br)
        xs = _moe(h, logits, xs, mod_l, moe_w1, moe_w3, moe_w2, l, final_g, l == DEPTH - 1)
    return xs.reshape(BATCH, SEQ, D_MODEL)
```

```python
import functools

import jax
import jax.numpy as jnp
import numpy as np
from jax import lax
from jax.experimental import pallas as pl
from jax.experimental.pallas import tpu as pltpu

D_MODEL = 1024
BATCH = 8
SEQ = 4096
DEPTH = 2
GRID_W = 64
CTX_LEN = 256
EPS = 1e-6
N_MOD = 6

A_HEADS = 8
A_KV_HEADS = 2
A_HEAD_DIM = 64
A_WIDTH = A_HEADS * A_HEAD_DIM
A_KV_WIDTH = A_KV_HEADS * A_HEAD_DIM
A_SCALE = A_HEAD_DIM ** -0.5
ROPE_THETA = 10000.0
ROPE_FREQS = A_HEAD_DIM // 4

B_GROUPS = 4
B_WIDTH = D_MODEL // 2
B_GROUP_DIM = B_WIDTH // B_GROUPS
POOL_WINDOWS = (2, 4, 8, 16)
POOL_PAD = 16

C_HEADS = 16
C_HEAD_DIM = D_MODEL // C_HEADS
C_WIDTH = C_HEADS * C_HEAD_DIM
C_SCALE = C_HEAD_DIM ** -0.5
NA_ROWS_MAX = 8
NA_COLS = 16
NA_BAND = 4
LOG2_E = 1.4426950408889634

N_GROUPS = 4
E_PER_GROUP = 8
N_EXPERTS = N_GROUPS * E_PER_GROUP
TOP_K = 2
D_EXPERT = D_MODEL // 2

LANES = 128
TOKEN_SUBLANES = D_MODEL // LANES
ROW_TILE = 256
MOE_ROWS = 512
DMA_UNROLL = 8
MOD_ROWS = 16
MOD_TILE_N = 1536
ROUTE_LANES = 128
ROUTE_EXPERT_ROW = 8
VMEM_LIMIT_BYTES = 48 * 1024 * 1024
MASKED = -1e30

F32 = jnp.float32
BF16 = jnp.bfloat16


def _params(*semantics):
    return pltpu.CompilerParams(dimension_semantics=semantics, vmem_limit_bytes=VMEM_LIMIT_BYTES)


def _split_bf16(a):
    hi = a.astype(BF16)
    lo = (a - hi.astype(F32)).astype(BF16)
    return hi, lo


def _dot(a, b):
    return jnp.dot(a, b, preferred_element_type=F32)


def _dot_nt(a, b):
    return lax.dot_general(a, b, (((1,), (1,)), ((), ())), preferred_element_type=F32)


def _silu(a):
    return a * jax.nn.sigmoid(a)


def _mod_kernel(c_ref, w_ref, b_ref, o_ref):
    s_hi, s_lo = _split_bf16(_silu(c_ref[...]))
    w_hi, w_lo = _split_bf16(w_ref[0])
    o_ref[0] = _dot(s_hi, w_hi) + _dot(s_hi, w_lo) + _dot(s_lo, w_hi) + b_ref[0]


def _modulation(cvec, ada_w, ada_b):
    depth, d, n = ada_w.shape
    return pl.pallas_call(
        _mod_kernel,
        grid=(depth, n // MOD_TILE_N),
        in_specs=[
            pl.BlockSpec((MOD_ROWS, d), lambda l, j: (0, 0)),
            pl.BlockSpec((1, d, MOD_TILE_N), lambda l, j: (l, 0, j)),
            pl.BlockSpec((1, 1, MOD_TILE_N), lambda l, j: (l, 0, j)),
        ],
        out_specs=pl.BlockSpec((1, MOD_ROWS, MOD_TILE_N), lambda l, j: (l, 0, j)),
        out_shape=jax.ShapeDtypeStruct((depth, MOD_ROWS, n), F32),
        compiler_params=_params("parallel", "parallel"),
        name="adaln_mod",
    )(cvec, ada_w, ada_b.reshape(depth, 1, n))


def _mod_spec(chunk, blocks_per_batch, ctx_row):
    def index(i):
        return (jnp.where(i % blocks_per_batch == 0, ctx_row, i // blocks_per_batch), 0, chunk)
    return pl.BlockSpec((1, 1, D_MODEL), index)


def _store_token_tiles(ref, rows):
    n = rows.shape[0]
    for s in range(TOKEN_SUBLANES):
        ref[pl.ds(s, n, stride=TOKEN_SUBLANES), :] = rows[:, s * LANES:(s + 1) * LANES]


def _load_token_tiles(ref, n, s):
    return ref[pl.ds(s, n, stride=TOKEN_SUBLANES), :]


def _stream_specs(stream):
    if not isinstance(stream, tuple):
        return [pl.BlockSpec((ROW_TILE, D_MODEL), lambda i: (i, 0))], [stream]
    assert CTX_LEN == ROW_TILE
    bpb = (CTX_LEN + SEQ) // ROW_TILE
    xpb = SEQ // ROW_TILE
    ctx_spec = pl.BlockSpec((ROW_TILE, D_MODEL), lambda i: (i // bpb, 0))
    x_spec = pl.BlockSpec((ROW_TILE, D_MODEL), lambda i: ((i // bpb) * xpb + jnp.maximum(i % bpb - 1, 0), 0))
    return [ctx_spec, x_spec], list(stream)


def _stream_rows(refs):
    if len(refs) == 1:
        return refs[0][...]
    is_ctx = pl.program_id(0) % ((CTX_LEN + SEQ) // ROW_TILE) == 0
    return jnp.where(is_ctx, refs[0][...], refs[1][...])


def _stream_len(stream):
    return sum(a.shape[0] for a in stream) if isinstance(stream, tuple) else stream.shape[0]


def _norm_modulate(x, g, shift, scale):
    ms = jnp.mean(x * x, axis=-1, keepdims=True)
    return (x * lax.rsqrt(ms + EPS) * g) * (1.0 + scale) + shift


def _head_mean_square(a):
    first = lax.broadcasted_iota(jnp.int32, (1, LANES), 1) < A_HEAD_DIM
    tiles = []
    for t in range(a.shape[-1] // LANES):
        sq = a[:, t * LANES:(t + 1) * LANES]
        sq = sq * sq
        lo = jnp.sum(jnp.where(first, sq, 0.0), axis=-1, keepdims=True)
        hi = jnp.sum(jnp.where(first, 0.0, sq), axis=-1, keepdims=True)
        tiles.append(jnp.where(first, lo, hi))
    return jnp.concatenate(tiles, axis=-1) * (1.0 / A_HEAD_DIM)


def _qk_norm_rope(a, perm_ref, cos_ref, sin_ref):
    w = a.shape[-1]
    rot = _dot(a.astype(BF16), perm_ref[:w, :w])
    return lax.rsqrt(_head_mean_square(a) + EPS) * (a * cos_ref[...] + rot * sin_ref[...])


def _in_even_kernel(n_x, *refs):
    x_refs = refs[:n_x]
    (g_ref, sh_ref, sc_ref, w_ref, cq_ref, sq_ref, ck_ref, sk_ref, perm_ref,
     q_ref, k_ref, v_ref, u_ref) = refs[n_x:]
    x = _stream_rows(x_refs)
    hb = _norm_modulate(x, g_ref[...], sh_ref[0], sc_ref[0]).astype(BF16)
    c0, c1, c2 = A_WIDTH, A_WIDTH + A_KV_WIDTH, A_WIDTH + 2 * A_KV_WIDTH
    q = _dot(hb, w_ref[:, :c0])
    q_ref[...] = _qk_norm_rope(q, perm_ref, cq_ref, sq_ref).astype(BF16)
    k = _qk_norm_rope(_dot(hb, w_ref[:, c0:c1]), perm_ref, ck_ref, sk_ref)
    v = _dot(hb, w_ref[:, c1:c2])
    ones_col = (lax.broadcasted_iota(jnp.int32, (x.shape[0], LANES - A_HEAD_DIM), 1) == 0).astype(F32)
    for h in range(A_KV_HEADS):
        lanes = slice(h * A_HEAD_DIM, (h + 1) * A_HEAD_DIM)
        k_ref[h] = k[:, lanes].astype(BF16)
        v_ref[h] = jnp.concatenate([v[:, lanes], ones_col], axis=-1).astype(BF16)
    u_ref[...] = _dot(hb, w_ref[:, c2:])


def _rope_tables(q_gain, k_gain):
    t = np.arange(SEQ)
    pos = np.stack([t // GRID_W, t % GRID_W], axis=-1).astype(np.float32)
    inv = (ROPE_THETA ** (-np.arange(ROPE_FREQS, dtype=np.float32) / ROPE_FREQS)).astype(np.float32)
    ang = pos[:, :, None] * inv
    cos = np.concatenate([np.cos(ang), np.cos(ang)], axis=-1).reshape(SEQ, A_HEAD_DIM)
    sin = np.concatenate([-np.sin(ang), np.sin(ang)], axis=-1).reshape(SEQ, A_HEAD_DIM)
    cos = np.concatenate([np.ones((CTX_LEN, A_HEAD_DIM), np.float32), cos.astype(np.float32)], axis=0)
    sin = np.concatenate([np.zeros((CTX_LEN, A_HEAD_DIM), np.float32), sin.astype(np.float32)], axis=0)
    swap = _rope_swap()
    def tables(gain, heads, scale):
        c = jnp.asarray(cos) * (gain * scale)[None, :]
        s = jnp.asarray(sin) * (gain[swap] * scale)[None, :]
        return jnp.tile(c, (1, heads)), jnp.tile(s, (1, heads))
    cq, sq = tables(q_gain, A_HEADS, A_SCALE * LOG2_E)
    ck, sk = tables(k_gain, A_KV_HEADS, 1.0)
    return cq, sq, ck, sk


def _rope_swap():
    j = np.arange(A_HEAD_DIM)
    half = (j // ROPE_FREQS) % 2
    return np.where(half == 0, j + ROPE_FREQS, j - ROPE_FREQS)


def _rotate_half_matrix():
    lane = np.arange(A_WIDTH)
    swap = np.tile(_rope_swap(), A_HEADS) + (lane // A_HEAD_DIM) * A_HEAD_DIM
    perm = np.zeros((A_WIDTH, A_WIDTH), np.float32)
    perm[swap, lane] = 1.0
    return jnp.asarray(perm, BF16)


def _in_even(stream, mod_l, norm_g, w_in, q_gain, k_gain):
    t_rows = _stream_len(stream)
    x_specs, x_args = _stream_specs(stream)
    bpb = (CTX_LEN + SEQ) // ROW_TILE
    cq, sq, ck, sk = _rope_tables(q_gain, k_gain)
    n_in = w_in.shape[1]
    row = lambda w: pl.BlockSpec((ROW_TILE, w), lambda i: (i, 0))
    tab = lambda w: pl.BlockSpec((ROW_TILE, w), lambda i: (i % bpb, 0))
    full = lambda a, b: pl.BlockSpec((a, b), lambda i: (0, 0))
    return pl.pallas_call(
        functools.partial(_in_even_kernel, len(x_specs)),
        grid=(t_rows // ROW_TILE,),
        in_specs=x_specs + [
            full(1, D_MODEL), _mod_spec(0, bpb, BATCH), _mod_spec(1, bpb, BATCH),
            full(D_MODEL, n_in), tab(A_WIDTH), tab(A_WIDTH), tab(A_KV_WIDTH), tab(A_KV_WIDTH),
            full(A_WIDTH, A_WIDTH),
        ],
        out_specs=[row(A_WIDTH),
                   pl.BlockSpec((A_KV_HEADS, ROW_TILE, A_HEAD_DIM), lambda i: (0, i, 0)),
                   pl.BlockSpec((A_KV_HEADS, ROW_TILE, LANES), lambda i: (0, i, 0)),
                   row(B_WIDTH)],
        out_shape=[
            jax.ShapeDtypeStruct((t_rows, A_WIDTH), BF16),
            jax.ShapeDtypeStruct((A_KV_HEADS, t_rows, A_HEAD_DIM), BF16),
            jax.ShapeDtypeStruct((A_KV_HEADS, t_rows, LANES), BF16),
            jax.ShapeDtypeStruct((t_rows, B_WIDTH), F32),
        ],
        compiler_params=_params("parallel"),
        name="even_in_proj",
    )(*x_args, norm_g.reshape(1, D_MODEL), mod_l, mod_l, w_in.astype(BF16), cq, sq, ck, sk, _rotate_half_matrix())


def _gqa_kernel(q_ref, k_ref, v_ref, o_ref):
    def attend(n_keys):
        outs = []
        for h in range(A_HEADS):
            g = h // (A_HEADS // A_KV_HEADS)
            s = _dot_nt(q_ref[:, h * A_HEAD_DIM:(h + 1) * A_HEAD_DIM], k_ref[g, :n_keys, :])
            m = jnp.max(s, axis=-1, keepdims=True)
            p = jnp.exp2(s - m).astype(BF16)
            acc = _dot(p, v_ref[g, :n_keys, :])
            outs.append(acc[:, :A_HEAD_DIM] / acc[:, A_HEAD_DIM:A_HEAD_DIM + 1])
        o_ref[...] = jnp.concatenate(outs, axis=-1).astype(BF16)

    is_ctx = pl.program_id(1) == 0

    @pl.when(is_ctx)
    def _():
        attend(CTX_LEN)

    @pl.when(jnp.logical_not(is_ctx))
    def _():
        attend(CTX_LEN + SEQ)


def _gqa(q, k_heads, v_heads):
    t_rows = q.shape[0]
    r = CTX_LEN + SEQ
    bpb = r // ROW_TILE
    return pl.pallas_call(
        _gqa_kernel,
        grid=(BATCH, bpb),
        in_specs=[
            pl.BlockSpec((ROW_TILE, A_WIDTH), lambda b, j: (b * bpb + j, 0)),
            pl.BlockSpec((A_KV_HEADS, r, A_HEAD_DIM), lambda b, j: (0, b, 0)),
            pl.BlockSpec((A_KV_HEADS, r, LANES), lambda b, j: (0, b, 0)),
        ],
        out_specs=pl.BlockSpec((ROW_TILE, A_WIDTH), lambda b, j: (b * bpb + j, 0)),
        out_shape=jax.ShapeDtypeStruct((t_rows, A_WIDTH), BF16),
        compiler_params=_params("parallel", "parallel"),
        name="gqa_attention",
    )(q, k_heads, v_heads)


def _pool_kernel(u_ref, w_ref, s_ref, o_ref, pad_ref):
    g = pl.program_id(1)
    w_bf = w_ref[0].astype(BF16)
    for start, length in ((0, CTX_LEN), (CTX_LEN, SEQ)):
        u = u_ref[start:start + length, :]
        pad_ref[0:POOL_PAD, :] = jnp.zeros((POOL_PAD, LANES), F32)
        pad_ref[POOL_PAD:POOL_PAD + length, :] = u
        pad_ref[POOL_PAD + length:2 * POOL_PAD + length, :] = jnp.zeros((POOL_PAD, LANES), F32)
        t = lax.broadcasted_iota(jnp.int32, (length, LANES), 0)
        for gi, win in enumerate(POOL_WINDOWS):
            @pl.when(g == gi)
            def _(win=win, u=u, t=t, start=start, length=length):
                base = POOL_PAD - win // 2
                acc = pad_ref[base:base + length, :]
                for j in range(1, win):
                    acc = acc + pad_ref[base + j:base + j + length, :]
                lo = jnp.clip(t - win // 2, 0, length)
                hi = jnp.clip(t - win // 2 + win, 0, length)
                pooled = acc / (hi - lo).astype(F32) - u
                mixed = _dot(pooled.astype(BF16), w_bf) * s_ref[0]
                o_ref[start:start + length, :] = mixed.astype(BF16)


def _pool(u, pool_w, pool_scale):
    t_rows = u.shape[0]
    r = CTX_LEN + SEQ
    return pl.pallas_call(
        _pool_kernel,
        grid=(BATCH, B_GROUPS),
        in_specs=[
            pl.BlockSpec((r, B_GROUP_DIM), lambda b, g: (b, g)),
            pl.BlockSpec((1, B_GROUP_DIM, B_GROUP_DIM), lambda b, g: (g, 0, 0)),
            pl.BlockSpec((1, 1, B_GROUP_DIM), lambda b, g: (g, 0, 0)),
        ],
        out_specs=pl.BlockSpec((r, B_GROUP_DIM), lambda b, g: (b, g)),
        out_shape=jax.ShapeDtypeStruct((t_rows, B_WIDTH), BF16),
        scratch_shapes=[pltpu.VMEM((SEQ + 2 * POOL_PAD, B_GROUP_DIM), F32)],
        compiler_params=_params("parallel", "parallel"),
        name="pool_mixer",
    )(u, pool_w, pool_scale.reshape(B_GROUPS, 1, B_GROUP_DIM))


def _in_odd_kernel(x_ref, g_ref, sh_ref, sc_ref, w_ref, q_ref, k_ref, v_ref):
    hb = _norm_modulate(x_ref[...], g_ref[...], sh_ref[0], sc_ref[0]).astype(BF16)
    q_ref[...] = (_dot(hb, w_ref[:, :C_WIDTH]) * (C_SCALE * LOG2_E)).astype(BF16)
    k_ref[...] = _dot(hb, w_ref[:, C_WIDTH:2 * C_WIDTH]).astype(BF16)
    v_ref[...] = _dot(hb, w_ref[:, 2 * C_WIDTH:]).astype(BF16)


def _in_odd(xs, mod_l, norm_g, w_in):
    t_rows = xs.shape[0]
    bpb = (CTX_LEN + SEQ) // ROW_TILE
    row = lambda w: pl.BlockSpec((ROW_TILE, w), lambda i: (i, 0))
    full = lambda a, b: pl.BlockSpec((a, b), lambda i: (0, 0))
    return pl.pallas_call(
        _in_odd_kernel,
        grid=(t_rows // ROW_TILE,),
        in_specs=[row(D_MODEL), full(1, D_MODEL), _mod_spec(0, bpb, BATCH), _mod_spec(1, bpb, BATCH),
                  full(D_MODEL, 3 * C_WIDTH)],
        out_specs=[row(C_WIDTH)] * 3,
        out_shape=[jax.ShapeDtypeStruct((t_rows, C_WIDTH), BF16)] * 3,
        compiler_params=_params("parallel"),
        name="odd_in_proj",
    )(xs, norm_g.reshape(1, D_MODEL), mod_l, mod_l, w_in.astype(BF16))


def _na_plan():
    rows = SEQ // GRID_W
    kh = min(NA_ROWS_MAX, rows)
    win_rows = min(kh + NA_BAND - 1, rows)
    starts, types, patterns = [], [], []
    for r0 in range(0, rows, NA_BAND):
        ws = int(np.clip(r0 - kh // 2, 0, rows - win_rows))
        pat = np.full((NA_BAND, win_rows), -1, np.int64)
        for rr in range(NA_BAND):
            r = r0 + rr
            rs = int(np.clip(r - kh // 2, 0, rows - kh))
            for wi in range(win_rows):
                if rs <= ws + wi < rs + kh:
                    pat[rr, wi] = ws + wi - r + (NA_ROWS_MAX - 1)
        for t, known in enumerate(patterns):
            if np.array_equal(known, pat):
                types.append(t)
                break
        else:
            types.append(len(patterns))
            patterns.append(pat)
        starts.append(ws)
    return win_rows, np.asarray(starts, np.int32), np.asarray(types, np.int32), patterns


def _na_strip_plan(patterns):
    n_off = 2 * NA_ROWS_MAX - 1
    cuts = []
    for pat in patterns:
        rows = []
        for rr in range(pat.shape[0]):
            valid = np.nonzero(pat[rr] >= 0)[0]
            lo, hi = int(valid[0]), int(valid[-1]) + 1
            delta = int(pat[rr, lo]) - lo
            assert np.array_equal(pat[rr, lo:hi], np.arange(lo, hi) + delta)
            rows.append((delta, lo, hi))
        cuts.append(rows)
    win_rows = patterns[0].shape[1]
    deltas = [d for rows in cuts for d, _, _ in rows]
    pad_lo = max(0, -min(deltas))
    pad_hi = max(0, max(deltas) + win_rows - n_off)
    pad_hi += (pad_lo + n_off + pad_hi) % 2
    plan = tuple(tuple((d + pad_lo, lo, hi) for d, lo, hi in rows) for rows in cuts)
    return plan, pad_lo, pad_hi


def _na_bias_strip(rel_bias, pad_lo, pad_hi):
    cq = np.arange(GRID_W)
    col_start = np.clip(cq - NA_COLS // 2, 0, GRID_W - NA_COLS)
    ck = np.arange(GRID_W)
    inside = (ck[None, :] >= col_start[:, None]) & (ck[None, :] < col_start[:, None] + NA_COLS)
    shift = GRID_W - NA_COLS
    padded = jnp.pad(rel_bias.astype(F32) * LOG2_E, ((0, 0), (0, 0), (shift, shift)))
    tcol = jnp.stack([padded[:, :, GRID_W - 1 - c:2 * GRID_W - 1 - c] for c in range(GRID_W)], axis=1)
    tcol = jnp.where(jnp.asarray(inside)[None, :, None, :], tcol, MASKED)
    strip = tcol.reshape(C_HEADS, GRID_W, -1)
    return jnp.pad(strip, ((0, 0), (0, 0), (pad_lo * GRID_W, pad_hi * GRID_W)), constant_values=MASKED)


def _na_kernel(plan, ws_ref, ty_ref, q_ref, k_ref, v_ref, strip_ref, o_ref, bias_ref):
    n_bands = (SEQ // GRID_W) // NA_BAND
    band_q = NA_BAND * GRID_W
    win = bias_ref.shape[-1]

    @pl.when(pl.program_id(1) == 0)
    def _():
        win_row = lax.broadcasted_iota(jnp.int32, (GRID_W, win), 1) // GRID_W
        for h in range(2):
            for t, rows in enumerate(plan):
                for rr, (block, lo, hi) in enumerate(rows):
                    cut = strip_ref[h, :, block * GRID_W:block * GRID_W + win]
                    own = jnp.logical_and(win_row >= lo, win_row < hi)
                    bias_ref[h, t, rr * GRID_W:(rr + 1) * GRID_W, :] = jnp.where(own, cut, MASKED)

    o_ref[0:CTX_LEN, :] = jnp.zeros((CTX_LEN, LANES), BF16)
    kc = k_ref[0:CTX_LEN, :]
    vc = v_ref[0:CTX_LEN, :]
    first = lax.broadcasted_iota(jnp.int32, (1, LANES), 1) < C_HEAD_DIM

    def one_band(b, carry):
        q_off = pl.multiple_of(CTX_LEN + b * band_q, GRID_W)
        k_off = pl.multiple_of(CTX_LEN + ws_ref[b] * GRID_W, GRID_W)
        qb = q_ref[pl.ds(q_off, band_q), :]
        kw = k_ref[pl.ds(k_off, win), :]
        vw = v_ref[pl.ds(k_off, win), :]
        outs = []
        for h in range(2):
            qm = jnp.where(first if h == 0 else jnp.logical_not(first), qb, jnp.zeros_like(qb))
            s_loc = _dot_nt(qm, kw) + bias_ref[h, ty_ref[b]]
            s_ctx = _dot_nt(qm, kc)
            m = jnp.maximum(jnp.max(s_loc, axis=-1, keepdims=True), jnp.max(s_ctx, axis=-1, keepdims=True))
            p_loc = jnp.exp2(s_loc - m)
            p_ctx = jnp.exp2(s_ctx - m)
            denom = jnp.sum(p_loc, axis=-1, keepdims=True) + jnp.sum(p_ctx, axis=-1, keepdims=True)
            o = _dot(p_loc.astype(BF16), vw) + _dot(p_ctx.astype(BF16), vc)
            outs.append(o / denom)
        o_ref[pl.ds(q_off, band_q), :] = jnp.where(first, outs[0], outs[1]).astype(BF16)
        return carry

    lax.fori_loop(0, n_bands, one_band, 0, unroll=2)


def _na(q, k, v, rel_bias):
    t_rows = q.shape[0]
    r = CTX_LEN + SEQ
    win_rows, starts, types, patterns = _na_plan()
    plan, pad_lo, pad_hi = _na_strip_plan(patterns)
    strip = _na_bias_strip(rel_bias, pad_lo, pad_hi)
    blk = pl.BlockSpec((r, LANES), lambda p, b, ws, ty: (b, p))
    return pl.pallas_call(
        functools.partial(_na_kernel, plan),
        grid_spec=pltpu.PrefetchScalarGridSpec(
            num_scalar_prefetch=2,
            grid=(C_HEADS // 2, BATCH),
            in_specs=[blk, blk, blk,
                      pl.BlockSpec((2, GRID_W, strip.shape[-1]), lambda p, b, ws, ty: (p, 0, 0))],
            out_specs=blk,
            scratch_shapes=[pltpu.VMEM((2, len(patterns), NA_BAND * GRID_W, win_rows * GRID_W), F32)],
        ),
        out_shape=jax.ShapeDtypeStruct((t_rows, C_WIDTH), BF16),
        compiler_params=_params("parallel", "arbitrary"),
        name="neighbourhood_attention",
    )(jnp.asarray(starts), jnp.asarray(types), q, k, v, strip)


def _out_kernel(n_lat, n_x, *refs):
    lat_refs = refs[:n_lat]
    w_ref = refs[n_lat]
    x_refs = refs[n_lat + 1:n_lat + 1 + n_x]
    g1_ref, g_ref, sh_ref, sc_ref, wr_ref, br_ref, xo_ref, h_ref, lg_ref = refs[n_lat + 1 + n_x:]
    y = None
    off = 0
    for lat in lat_refs:
        width = lat.shape[-1]
        part = _dot(lat[...], w_ref[off:off + width, :])
        y = part if y is None else y + part
        off += width
    x_new = _stream_rows(x_refs) + g1_ref[0] * y
    xo_ref[...] = x_new
    h = _norm_modulate(x_new, g_ref[...], sh_ref[0], sc_ref[0])
    _store_token_tiles(h_ref, h)
    h_hi, h_lo = _split_bf16(h)
    both = _dot(h_hi, wr_ref[...])
    lg_ref[...] = (both[:, :ROUTE_LANES] + both[:, ROUTE_LANES:]
                   + _dot(h_lo, wr_ref[:, :ROUTE_LANES]) + br_ref[...])


def _router_weights(wg, bg, we, be):
    gap = ROUTE_EXPERT_ROW - N_GROUPS
    tail = ROUTE_LANES - ROUTE_EXPERT_ROW - N_EXPERTS
    d = wg.shape[0]
    w = jnp.concatenate([wg, jnp.zeros((d, gap), F32), we, jnp.zeros((d, tail), F32)], axis=1)
    w_hi, w_lo = _split_bf16(w)
    b = jnp.concatenate([bg, jnp.zeros((gap,), F32), be, jnp.zeros((tail,), F32)])
    return jnp.concatenate([w_hi, w_lo], axis=1), b.reshape(1, ROUTE_LANES)


def _out_proj(lats, w_out, stream, mod_l, norm_g, wr, br):
    t_rows = _stream_len(stream)
    x_specs, x_args = _stream_specs(stream)
    bpb = (CTX_LEN + SEQ) // ROW_TILE
    row = lambda w: pl.BlockSpec((ROW_TILE, w), lambda i: (i, 0))
    full = lambda a, b: pl.BlockSpec((a, b), lambda i: (0, 0))
    return pl.pallas_call(
        functools.partial(_out_kernel, len(lats), len(x_specs)),
        grid=(t_rows // ROW_TILE,),
        in_specs=[row(l.shape[1]) for l in lats] + [full(*w_out.shape)] + x_specs + [
            _mod_spec(2, bpb, BATCH), full(1, D_MODEL),
            _mod_spec(3, bpb, BATCH), _mod_spec(4, bpb, BATCH), full(D_MODEL, 2 * ROUTE_LANES), full(1, ROUTE_LANES),
        ],
        out_specs=[row(D_MODEL), pl.BlockSpec((ROW_TILE * TOKEN_SUBLANES, LANES), lambda i: (i, 0)),
                   row(ROUTE_LANES)],
        out_shape=[
            jax.ShapeDtypeStruct((t_rows, D_MODEL), F32),
            jax.ShapeDtypeStruct((t_rows * TOKEN_SUBLANES, LANES), F32),
            jax.ShapeDtypeStruct((t_rows, ROUTE_LANES), F32),
        ],
        compiler_params=_params("parallel"),
        name="out_proj_ffn_norm",
    )(*lats, w_out.astype(BF16), *x_args, mod_l, norm_g.reshape(1, D_MODEL), mod_l, mod_l, wr, br)


def _route_kernel(lg_ref, tri_ref, e_ref, rk_ref, wt_ref, cnt_ref, run_ref):
    @pl.when(pl.program_id(0) == 0)
    def _():
        run_ref[...] = jnp.zeros(run_ref.shape, F32)

    lt = lg_ref[...].T
    sub = lax.broadcasted_iota(jnp.int32, (E_PER_GROUP, ROW_TILE), 0)
    gl = jnp.where(sub < N_GROUPS, lt[0:E_PER_GROUP], -jnp.inf)
    g_max = jnp.max(gl, axis=0, keepdims=True)
    g_idx = jnp.min(jnp.where(gl == g_max, sub, E_PER_GROUP), axis=0, keepdims=True)
    g_w = 1.0 / jnp.sum(jnp.exp(gl - g_max), axis=0, keepdims=True)
    sel = lt[ROUTE_EXPERT_ROW:ROUTE_EXPERT_ROW + E_PER_GROUP]
    for g in range(1, N_GROUPS):
        lo = ROUTE_EXPERT_ROW + g * E_PER_GROUP
        sel = jnp.where(g_idx == g, lt[lo:lo + E_PER_GROUP], sel)
    m1 = jnp.max(sel, axis=0, keepdims=True)
    i1 = jnp.min(jnp.where(sel == m1, sub, E_PER_GROUP), axis=0, keepdims=True)
    rest = jnp.where(sub == i1, -jnp.inf, sel)
    m2 = jnp.max(rest, axis=0, keepdims=True)
    i2 = jnp.min(jnp.where(rest == m2, sub, E_PER_GROUP), axis=0, keepdims=True)
    t = jnp.exp(m2 - m1)
    p1 = 1.0 / (1.0 + t)
    wt_ref[0] = jnp.concatenate([g_w * p1, g_w * (t * p1)], axis=0)
    e_flat = jnp.concatenate([g_idx * E_PER_GROUP + i1, g_idx * E_PER_GROUP + i2], axis=1)
    e_ref[0] = e_flat
    hit = lax.broadcasted_iota(jnp.int32, (N_EXPERTS, TOP_K * ROW_TILE), 0) == e_flat
    onehot = jnp.where(hit, 1.0, 0.0)
    before = _dot(onehot.astype(BF16), tri_ref[...]) + run_ref[:, 0:1]
    rk_ref[0] = jnp.sum(jnp.where(hit, before, 0.0), axis=0, keepdims=True).astype(jnp.int32)
    run_ref[...] = run_ref[...] + jnp.sum(onehot, axis=1, keepdims=True)
    cnt_ref[...] = run_ref[...]


def _route(logits):
    t_rows = logits.shape[0]
    n_blk = t_rows // ROW_TILE
    n_asg = TOP_K * ROW_TILE
    tri = jnp.asarray(np.triu(np.ones((n_asg, n_asg), np.float32), 1), BF16)
    e_blk, rank_blk, wt_blk, counts = pl.pallas_call(
        _route_kernel,
        grid=(n_blk,),
        in_specs=[pl.BlockSpec((ROW_TILE, ROUTE_LANES), lambda i: (i, 0)),
                  pl.BlockSpec((n_asg, n_asg), lambda i: (0, 0))],
        out_specs=[pl.BlockSpec((1, 1, n_asg), lambda i: (i, 0, 0)),
                   pl.BlockSpec((1, 1, n_asg), lambda i: (i, 0, 0)),
                   pl.BlockSpec((1, TOP_K, ROW_TILE), lambda i: (i, 0, 0)),
                   pl.BlockSpec((N_EXPERTS, LANES), lambda i: (0, 0))],
        out_shape=[jax.ShapeDtypeStruct((n_blk, 1, n_asg), jnp.int32),
                   jax.ShapeDtypeStruct((n_blk, 1, n_asg), jnp.int32),
                   jax.ShapeDtypeStruct((n_blk, TOP_K, ROW_TILE), F32),
                   jax.ShapeDtypeStruct((N_EXPERTS, LANES), F32)],
        scratch_shapes=[pltpu.VMEM((N_EXPERTS, LANES), F32)],
        compiler_params=_params("arbitrary"),
        name="moe_route",
    )(logits, tri)
    counts = counts[:, 0].astype(jnp.int32)
    blocks_of = (counts + MOE_ROWS - 1) // MOE_ROWS
    b_end = jnp.cumsum(blocks_of)
    slot_base = ((b_end - blocks_of) * MOE_ROWS).astype(jnp.int32)
    n_used = b_end[-1].astype(jnp.int32)
    blk = jnp.minimum(jnp.arange(_moe_blocks(t_rows), dtype=jnp.int32), n_used - 1)
    block_e = jnp.minimum(jnp.sum((b_end[None, :] <= blk[:, None]).astype(jnp.int32), axis=1), N_EXPERTS - 1)
    weights = wt_blk.transpose(0, 2, 1).reshape(t_rows, TOP_K)
    return (slot_base, counts, e_blk, rank_blk), weights, block_e, n_used.reshape(1)


def _moe_blocks(t_rows):
    return (t_rows * TOP_K) // MOE_ROWS + N_EXPERTS


def _tile_copy(src_ref, src_token, dst_ref, dst_token, sem):
    src = src_ref.at[pl.ds(pl.multiple_of(src_token * TOKEN_SUBLANES, TOKEN_SUBLANES), TOKEN_SUBLANES)]
    dst = dst_ref.at[pl.ds(pl.multiple_of(dst_token * TOKEN_SUBLANES, TOKEN_SUBLANES), TOKEN_SUBLANES)]
    return pltpu.make_async_copy(src, dst, sem)


def _start_rows(copies):
    def start(r, c):
        for n, cp in enumerate(copies(r)):
            cp.start(priority=n % 2)
        return c

    lax.fori_loop(0, ROW_TILE, start, 0, unroll=DMA_UNROLL)


def _wait_rows(copies):
    def wait(r, c):
        for cp in copies(r):
            cp.wait()
        return c

    lax.fori_loop(0, ROW_TILE, wait, 0, unroll=DMA_UNROLL)


def _start_then_wait(copies):
    _start_rows(copies)
    _wait_rows(copies)


def _slot(base_ref, e_ref, rk_ref, j):
    return base_ref[e_ref[0, 0, j]] + rk_ref[0, 0, j]


def _dispatch_kernel(base_ref, cnt_ref, e_ref, rk_ref, h_ref, xpad_ref, zero_ref, sem):
    _start_then_wait(lambda r: [
        _tile_copy(h_ref, r, xpad_ref, _slot(base_ref, e_ref, rk_ref, k * ROW_TILE + r), sem)
        for k in range(TOP_K)])

    @pl.when(pl.program_id(0) == pl.num_programs(0) - 1)
    def _():
        zero_ref[...] = jnp.zeros(zero_ref.shape, F32)
        block_end = lambda e: base_ref[e] + (cnt_ref[e] + MOE_ROWS - 1) // MOE_ROWS * MOE_ROWS

        def pad_expert(act, e, c):
            lo = base_ref[e] + cnt_ref[e]
            n = block_end(e) - lo
            size = MOE_ROWS // 2
            while size >= 1:
                at = pl.multiple_of((lo + (n & ~(2 * size - 1))) * TOKEN_SUBLANES, TOKEN_SUBLANES)
                cp = pltpu.make_async_copy(zero_ref.at[pl.ds(0, size * TOKEN_SUBLANES)],
                                           xpad_ref.at[pl.ds(at, size * TOKEN_SUBLANES)], sem)
                pl.when((n & size) != 0)(functools.partial(act, cp))
                size //= 2
            return c

        lax.fori_loop(0, N_EXPERTS, functools.partial(pad_expert, lambda cp: cp.start()), 0)
        lax.fori_loop(0, N_EXPERTS, functools.partial(pad_expert, lambda cp: cp.wait()), 0)
        block_rows = MOE_ROWS * TOKEN_SUBLANES
        first_unused = block_end(N_EXPERTS - 1) // MOE_ROWS
        n_blocks = xpad_ref.shape[0] // block_rows
        block_copy = lambda b: pltpu.make_async_copy(
            zero_ref, xpad_ref.at[pl.ds(pl.multiple_of(b * block_rows, block_rows), block_rows)], sem)
        lax.fori_loop(first_unused, n_blocks, lambda b, cc: (block_copy(b).start(), cc)[1], 0)
        lax.fori_loop(first_unused, n_blocks, lambda b, cc: (block_copy(b).wait(), cc)[1], 0)


def _dispatch(h_tiles, slots):
    slot_base, counts, e_blk, rank_blk = slots
    t_rows = h_tiles.shape[0] // TOKEN_SUBLANES
    n_pad = _moe_blocks(t_rows) * MOE_ROWS
    asg = pl.BlockSpec((1, 1, TOP_K * ROW_TILE), lambda i, base, cnt: (i, 0, 0), memory_space=pltpu.SMEM)
    return pl.pallas_call(
        _dispatch_kernel,
        grid_spec=pltpu.PrefetchScalarGridSpec(
            num_scalar_prefetch=2,
            grid=(t_rows // ROW_TILE,),
            in_specs=[asg, asg,
                      pl.BlockSpec((ROW_TILE * TOKEN_SUBLANES, LANES), lambda i, base, cnt: (i, 0))],
            out_specs=pl.BlockSpec(memory_space=pl.ANY),
            scratch_shapes=[pltpu.VMEM((MOE_ROWS * TOKEN_SUBLANES, LANES), F32), pltpu.SemaphoreType.DMA(())],
        ),
        out_shape=jax.ShapeDtypeStruct((n_pad * TOKEN_SUBLANES, LANES), F32),
        compiler_params=_params("arbitrary"),
        name="moe_dispatch",
    )(slot_base, counts, e_blk, rank_blk, h_tiles)


def _expert_kernel(be_ref, nu_ref, x_ref, w1_ref, w3_ref, w2_ref, y_ref, w1_bf, w3_bf, w2_bf):
    i = pl.program_id(0)
    fresh = jnp.logical_or(i == 0, be_ref[i] != be_ref[jnp.maximum(i - 1, 0)])

    @pl.when(fresh)
    def _():
        w1_bf[...] = w1_ref[0, 0].astype(BF16)
        w3_bf[...] = w3_ref[0, 0].astype(BF16)
        w2_bf[...] = w2_ref[0, 0].astype(BF16)

    used = i < nu_ref[0]

    @pl.when(used)
    def _():
        xb = jnp.concatenate(
            [_load_token_tiles(x_ref, MOE_ROWS, s).astype(BF16) for s in range(TOKEN_SUBLANES)], axis=-1)
        hid = _silu(_dot(xb, w1_bf[...])) * _dot(xb, w3_bf[...])
        _store_token_tiles(y_ref, _dot(hid.astype(BF16), w2_bf[...]))

    @pl.when(jnp.logical_not(used))
    def _():
        y_ref[...] = jnp.zeros(y_ref.shape, F32)


def _experts(xpad, block_e, n_used, w1, w3, w2, layer):
    g_blocks = xpad.shape[0] // (MOE_ROWS * TOKEN_SUBLANES)
    tile_rows = MOE_ROWS * TOKEN_SUBLANES
    x_index = lambda i, be, nu: (jnp.minimum(i, nu[0] - 1), 0)
    w_index = lambda i, be, nu: (layer, be[i], 0, 0)
    return pl.pallas_call(
        _expert_kernel,
        grid_spec=pltpu.PrefetchScalarGridSpec(
            num_scalar_prefetch=2,
            grid=(g_blocks,),
            in_specs=[
                pl.BlockSpec((tile_rows, LANES), x_index),
                pl.BlockSpec((1, 1, D_MODEL, D_EXPERT), w_index),
                pl.BlockSpec((1, 1, D_MODEL, D_EXPERT), w_index),
                pl.BlockSpec((1, 1, D_EXPERT, D_MODEL), w_index),
            ],
            out_specs=pl.BlockSpec((tile_rows, LANES), lambda i, be, nu: (i, 0)),
            scratch_shapes=[
                pltpu.VMEM((D_MODEL, D_EXPERT), BF16),
                pltpu.VMEM((D_MODEL, D_EXPERT), BF16),
                pltpu.VMEM((D_EXPERT, D_MODEL), BF16),
            ],
        ),
        out_shape=jax.ShapeDtypeStruct((g_blocks * tile_rows, LANES), F32),
        compiler_params=_params("arbitrary"),
        name="moe_experts",
    )(block_e, n_used, xpad, w1, w3, w2)


def _combine_kernel(final, base_ref, e_ref, rk_ref, e_next, rk_next, x_ref, wt_ref, g2_ref, fg_ref, ypad_ref,
                    o_ref, ybuf, sems):
    i = pl.program_id(0)
    slot = i % 2

    def gathers(e, rk, into):
        return lambda r: [
            _tile_copy(ypad_ref, _slot(base_ref, e, rk, k * ROW_TILE + r), ybuf.at[into, k], r, sems.at[into])
            for k in range(TOP_K)]

    @pl.when(i == 0)
    def _():
        _start_rows(gathers(e_ref, rk_ref, slot))

    @pl.when(i + 1 < pl.num_programs(0))
    def _():
        _start_rows(gathers(e_next, rk_next, 1 - slot))

    _wait_rows(gathers(e_ref, rk_ref, slot))
    wt = wt_ref[...]
    parts = []
    for s in range(TOKEN_SUBLANES):
        lanes = slice(s * LANES, (s + 1) * LANES)
        f = (wt[:, 0:1] * _load_token_tiles(ybuf.at[slot, 0], ROW_TILE, s)
             + wt[:, 1:2] * _load_token_tiles(ybuf.at[slot, 1], ROW_TILE, s))
        parts.append(x_ref[:, lanes] + g2_ref[0][:, lanes] * f)
    x_new = jnp.concatenate(parts, axis=-1)
    if final:
        ms = jnp.mean(x_new * x_new, axis=-1, keepdims=True)
        x_new = x_new * lax.rsqrt(ms + EPS) * fg_ref[...]
    o_ref[...] = x_new


def _combine(ypad, slots, weights, xs, mod_l, final_g, final):
    slot_base, _, e_blk, rank_blk = slots
    t_rows = xs.shape[0]
    bpb = (CTX_LEN + SEQ) // ROW_TILE
    if final:
        xpb = SEQ // ROW_TILE
        n_out = BATCH * xpb
        blk = lambda i: (i // xpb) * bpb + i % xpb + 1
        g2_index = lambda i, base: (i // xpb, 0, 5)
    else:
        n_out = t_rows // ROW_TILE
        blk = lambda i: i
        g2_index = lambda i, base: (jnp.where(i % bpb == 0, BATCH, i // bpb), 0, 5)
    asg = pl.BlockSpec((1, 1, TOP_K * ROW_TILE), lambda i, base: (blk(i), 0, 0), memory_space=pltpu.SMEM)
    asg_next = pl.BlockSpec((1, 1, TOP_K * ROW_TILE), lambda i, base: (blk(jnp.minimum(i + 1, n_out - 1)), 0, 0),
                            memory_space=pltpu.SMEM)
    return pl.pallas_call(
        functools.partial(_combine_kernel, final),
        grid_spec=pltpu.PrefetchScalarGridSpec(
            num_scalar_prefetch=1,
            grid=(n_out,),
            in_specs=[
                asg, asg, asg_next, asg_next,
                pl.BlockSpec((ROW_TILE, D_MODEL), lambda i, base: (blk(i), 0)),
                pl.BlockSpec((ROW_TILE, TOP_K), lambda i, base: (blk(i), 0)),
                pl.BlockSpec((1, 1, D_MODEL), g2_index),
                pl.BlockSpec((1, D_MODEL), lambda i, base: (0, 0)),
                pl.BlockSpec(memory_space=pl.ANY),
            ],
            out_specs=pl.BlockSpec((ROW_TILE, D_MODEL), lambda i, base: (i, 0)),
            scratch_shapes=[pltpu.VMEM((2, TOP_K, ROW_TILE * TOKEN_SUBLANES, LANES), F32),
                            pltpu.SemaphoreType.DMA((2,))],
        ),
        out_shape=jax.ShapeDtypeStruct((n_out * ROW_TILE, D_MODEL), F32),
        compiler_params=_params("arbitrary"),
        name="moe_combine_final" if final else "moe_combine",
    )(slot_base, e_blk, rank_blk, e_blk, rank_blk, xs, weights, mod_l, final_g.reshape(1, D_MODEL), ypad)


def _moe(h_tiles, logits, xs, mod_l, w1, w3, w2, layer, final_g, final):
    slots, weights, block_e, n_used = _route(logits)
    xpad = _dispatch(h_tiles, slots)
    ypad = _experts(xpad, block_e, n_used, w1, w3, w2, layer)
    return _combine(ypad, slots, weights, xs, mod_l, final_g, final)


def kernel(x, c, ctx, c_ctx, ada_w, ada_b, norm_mix_g, norm_ffn_g, even_w_in, even_w_out, a_q_gain, a_k_gain, pool_w, pool_scale, odd_w_in, odd_w_out, na_rel_bias, moe_w_group, moe_b_group, moe_w_expert, moe_b_expert, moe_w1, moe_w3, moe_w2, final_g):
    xs = (ctx.reshape(BATCH * CTX_LEN, D_MODEL), x.reshape(BATCH * SEQ, D_MODEL))
    cvec = jnp.concatenate([c, c_ctx[None, :], jnp.zeros((MOD_ROWS - BATCH - 1, D_MODEL), F32)], axis=0)
    mod = _modulation(cvec, ada_w, ada_b)
    for l in range(DEPTH):
        mod_l = mod[l].reshape(MOD_ROWS, 1, N_MOD * D_MODEL)
        i = l // 2
        if l % 2 == 0:
            q, k, v, u = _in_even(xs, mod_l, norm_mix_g[l], even_w_in[i], a_q_gain[i], a_k_gain[i])
            lats = [_gqa(q, k, v), _pool(u, pool_w[i], pool_scale[i])]
            w_out = even_w_out[i]
        else:
            q, k, v = _in_odd(xs, mod_l, norm_mix_g[l], odd_w_in[i])
            lats = [_na(q, k, v, na_rel_bias[i])]
            w_out = odd_w_out[i]
        wr, br = _router_weights(moe_w_group[l], moe_b_group[l], moe_w_expert[l], moe_b_expert[l])
        xs, h, logits = _out_proj(lats, w_out, xs, mod_l, norm_ffn_g[l], wr, br)
        xs = _moe(h, logits, xs, mod_l, moe_w1, moe_w3, moe_w2, l, final_g, l == DEPTH - 1)
    return xs.reshape(BATCH, SEQ, D_MODEL)
```
